```python
import jax
import jax.numpy as jnp
from jax import lax
import numpy as np

D_MODEL = 1024
BATCH = 8
SEQ = 2048
DEPTH = 2
DEC_BATCH = 8
DEC_SEQ = 8192
PAST_LEN = 128

EPS = 1e-6
NEG_INF = -1e30
BLOCK = 128
A_HEADS = 8
A_KV_HEADS = 2
A_GROUP = A_HEADS // A_KV_HEADS
A_HEAD_DIM = 64
WINDOW = 128
B_HEADS = 8
B_NOPE = 64
B_ROPE = 32
B_QK = B_NOPE + B_ROPE
B_V = 64
Q_LORA = 256
KV_LORA = 256
ROPE_BASE = 10000.0
IN_SIZES = (A_HEADS * A_HEAD_DIM, A_KV_HEADS * A_HEAD_DIM, A_KV_HEADS * A_HEAD_DIM, Q_LORA, KV_LORA, B_ROPE, D_MODEL, D_MODEL)
D_IN = A_HEADS * A_HEAD_DIM + 2 * A_KV_HEADS * A_HEAD_DIM + Q_LORA + KV_LORA + B_ROPE + 2 * D_MODEL
A_WIDTH = A_HEADS * A_HEAD_DIM
B_WIDTH = B_HEADS * B_V
P_HEADS = 8
P_KEY_DIM = 256
P_HALF = P_KEY_DIM // 2
N_KEYS = 128
N_EXPERTS = N_KEYS * N_KEYS
P_TOPK = 16
P_CHUNK = 128

kernel_name = 'hybrid_swa_mla_peer_adaln_encoder'


def _rmsnorm(x, g):
    xf = x.astype(jnp.float32)
    y = xf * lax.rsqrt(jnp.mean(xf * xf, axis=-1, keepdims=True) + EPS)
    return (y * g.astype(jnp.float32)).astype(x.dtype)


def _split_cols(z):
    out, start = [], 0
    for n in IN_SIZES:
        out.append(z[..., start:start + n])
        start += n
    return out


def _rope(x, pos):
    half = x.shape[-1] // 2
    inv = jnp.power(jnp.float32(ROPE_BASE), -jnp.arange(half, dtype=jnp.float32) / half)
    ang = pos.astype(jnp.float32)[:, None] * inv[None, :]
    cos = jnp.cos(ang)[None, :, None, :].astype(x.dtype)
    sin = jnp.sin(ang)[None, :, None, :].astype(x.dtype)
    x1, x2 = x[..., :half], x[..., half:]
    return jnp.concatenate([x1 * cos - x2 * sin, x1 * sin + x2 * cos], axis=-1)


def _window_gqa(q, k, v, sink):
    B, S = q.shape[0], q.shape[1]
    nb = S // BLOCK
    qb = q.reshape(B, nb, BLOCK, A_KV_HEADS, A_GROUP, A_HEAD_DIM)

    def band(t):
        tp = jnp.pad(t, ((0, 0), (BLOCK, BLOCK), (0, 0), (0, 0)))
        tp = tp.reshape(B, nb + 2, BLOCK, A_KV_HEADS, A_HEAD_DIM)
        return jnp.concatenate([tp[:, :-2], tp[:, 1:-1], tp[:, 2:]], axis=2)

    kb, vb = band(k), band(v)
    scale = A_HEAD_DIM ** -0.5
    logits = jnp.einsum('bnqhgd,bnkhd->bnhgqk', qb, kb).astype(jnp.float32) * scale
    qi = jnp.arange(BLOCK)[:, None] + BLOCK
    kj = jnp.arange(3 * BLOCK)[None, :]
    dist = jnp.abs(qi - kj)
    kabs = jnp.arange(nb)[:, None] * BLOCK - BLOCK + jnp.arange(3 * BLOCK)[None, :]
    valid = (dist <= WINDOW)[None] & ((kabs >= 0) & (kabs < S))[:, None, :]
    slopes = jnp.exp2(-8.0 * jnp.arange(1, A_HEADS + 1, dtype=jnp.float32) / A_HEADS)
    bias = -slopes.reshape(A_KV_HEADS, A_GROUP, 1, 1) * dist.astype(jnp.float32)
    logits = jnp.where(valid[None, :, None, None], logits + bias, NEG_INF)
    sink_col = jnp.broadcast_to(sink.astype(jnp.float32).reshape(A_KV_HEADS, A_GROUP, 1, 1), logits.shape[:-1] + (1,))
    p = jax.nn.softmax(jnp.concatenate([logits, sink_col], axis=-1), axis=-1)[..., :-1]
    o = jnp.einsum('bnhgqk,bnkhd->bnqhgd', p.astype(v.dtype), vb)
    return o.reshape(B, S, A_WIDTH)


def _mla(cq, ckv, kr, cq_g, ckv_g, w_uq, w_ukv, q_g, k_g):
    B, S = cq.shape[0], cq.shape[1]
    nb = S // BLOCK
    q = (_rmsnorm(cq, cq_g) @ w_uq).reshape(B, S, B_HEADS, B_QK)
    kv = (_rmsnorm(ckv, ckv_g) @ w_ukv).reshape(B, S, B_HEADS, B_NOPE + B_V)
    k_nope, v = kv[..., :B_NOPE], kv[..., B_NOPE:]
    k = jnp.concatenate([k_nope, jnp.broadcast_to(kr[:, :, None, :], (B, S, B_HEADS, B_ROPE))], axis=-1)
    q = _rmsnorm(q, q_g)
    k = _rmsnorm(k, k_g)
    pos = jnp.arange(S)
    q = jnp.concatenate([q[..., :B_NOPE], _rope(q[..., B_NOPE:], pos)], axis=-1)
    k = jnp.concatenate([k[..., :B_NOPE], _rope(k[..., B_NOPE:], pos)], axis=-1)
    scale = B_QK ** -0.5
    qb = q.reshape(B, nb, BLOCK, B_HEADS, B_QK).transpose(1, 0, 2, 3, 4)

    def one_block(qblk):
        s = jnp.einsum('bqhd,bkhd->bhqk', qblk, k).astype(jnp.float32) * scale
        p = jax.nn.softmax(s, axis=-1).astype(v.dtype)
        return jnp.einsum('bhqk,bkhd->bqhd', p, v)

    o = lax.map(one_block, qb)
    return o.transpose(1, 0, 2, 3, 4).reshape(B, S, B_WIDTH)


def _mixer(h, w_in, a_qg, a_kg, a_sink, cq_g, ckv_g, w_uq, w_ukv, b_qg, b_kg, w_pa, w_pb, w_out):
    B, S = h.shape[0], h.shape[1]
    z = h @ w_in
    qa, ka, va, cq, ckv, kr, ga, gb = _split_cols(z)
    qa = _rmsnorm(qa.reshape(B, S, A_HEADS, A_HEAD_DIM), a_qg)
    ka = _rmsnorm(ka.reshape(B, S, A_KV_HEADS, A_HEAD_DIM), a_kg)
    va = va.reshape(B, S, A_KV_HEADS, A_HEAD_DIM)
    ya = _window_gqa(qa, ka, va, a_sink)
    yb = _mla(cq, ckv, kr, cq_g, ckv_g, w_uq, w_ukv, b_qg, b_kg)
    m = jax.nn.sigmoid(ga) * (ya @ w_pa) + jax.nn.sigmoid(gb) * (yb @ w_pb)
    return m @ w_out


def _peer(h, wq, k1, k2, u, v):
    B, S, D = h.shape
    hc = h.reshape(-1, P_CHUNK, D)

    def chunk(hx):
        C = hx.shape[0]
        q = (hx @ wq).reshape(C, P_HEADS, 2, P_HALF)
        s1 = jnp.einsum('chd,kd->chk', q[:, :, 0], k1).astype(jnp.float32)
        s2 = jnp.einsum('chd,kd->chk', q[:, :, 1], k2).astype(jnp.float32)
        v1, i1 = lax.top_k(s1, P_TOPK)
        v2, i2 = lax.top_k(s2, P_TOPK)
        cand = (v1[..., :, None] + v2[..., None, :]).reshape(C, P_HEADS, P_TOPK * P_TOPK)
        cidx = (i1[..., :, None] * N_KEYS + i2[..., None, :]).reshape(C, P_HEADS, P_TOPK * P_TOPK)
        sc, sel = lax.top_k(cand, P_TOPK)
        eidx = jnp.take_along_axis(cidx, sel, axis=-1)
        g = jax.nn.softmax(sc, axis=-1)
        ue = jnp.take(u, eidx, axis=0)
        ve = jnp.take(v, eidx, axis=0)
        a = jax.nn.gelu(jnp.einsum('chkd,cd->chk', ue, hx), approximate=False)
        return jnp.einsum('chk,chkd->cd', g.astype(hx.dtype) * a, ve)

    return lax.map(chunk, hc).reshape(B, S, D)


def _trunk(x, c, ada_w, ada_b, norm1_g, norm2_g, w_in, a_q_norm, a_k_norm, a_sink, cq_norm, ckv_norm, w_uq, w_ukv, b_q_norm, b_k_norm, w_pa, w_pb, w_out, peer_wq, peer_k1, peer_k2, peer_u, peer_v):
    for l in range(DEPTH):
        mod = jax.nn.silu(c) @ ada_w[l] + ada_b[l]
        sh1, sc1, g1, sh2, sc2, g2 = jnp.split(mod[:, None, :], 6, axis=-1)
        h = _rmsnorm(x, norm1_g[l]) * (1 + sc1) + sh1
        x = x + g1 * _mixer(h, w_in[l], a_q_norm[l], a_k_norm[l], a_sink[l], cq_norm[l], ckv_norm[l], w_uq[l], w_ukv[l], b_q_norm[l], b_k_norm[l], w_pa[l], w_pb[l], w_out[l])
        h = _rmsnorm(x, norm2_g[l]) * (1 + sc2) + sh2
        x = x + g2 * _peer(h, peer_wq[l], peer_k1[l], peer_k2[l], peer_u[l], peer_v[l])
    return x


def setup_inputs(seed: int = 0) -> dict:
    key = jax.random.key(seed)
    ks = jax.random.split(key, 32)

    def nrm(k, shape, s):
        return jax.random.normal(k, shape, jnp.float32) * s

    def gain(k, shape):
        return 1.0 + 0.05 * jax.random.normal(k, shape, jnp.float32)

    D = D_MODEL
    return {
        'x_prompt': nrm(ks[0], (BATCH, SEQ, D), 1.0),
        'x_sample': nrm(ks[1], (DEC_BATCH, DEC_SEQ, D), 1.0),
        'c_prompt': nrm(ks[2], (BATCH, D), 1.0),
        'c_sample': nrm(ks[3], (DEC_BATCH, D), 1.0),
        'ada_w': nrm(ks[4], (DEPTH, D, 6 * D), 0.5 * D ** -0.5),
        'ada_b': nrm(ks[5], (DEPTH, 6 * D), 0.01),
        'norm1_g': gain(ks[6], (DEPTH, D)),
        'norm2_g': gain(ks[7], (DEPTH, D)),
        'w_in': nrm(ks[8], (DEPTH, D, D_IN), D ** -0.5),
        'a_q_norm': gain(ks[9], (DEPTH, A_HEAD_DIM)),
        'a_k_norm': gain(ks[10], (DEPTH, A_HEAD_DIM)),
        'a_sink': nrm(ks[11], (DEPTH, A_HEADS), 0.5),
        'cq_norm': gain(ks[12], (DEPTH, Q_LORA)),
        'ckv_norm': gain(ks[13], (DEPTH, KV_LORA)),
        'w_uq': nrm(ks[14], (DEPTH, Q_LORA, B_HEADS * B_QK), Q_LORA ** -0.5),
        'w_ukv': nrm(ks[15], (DEPTH, KV_LORA, B_HEADS * (B_NOPE + B_V)), KV_LORA ** -0.5),
        'b_q_norm': gain(ks[16], (DEPTH, B_QK)),
        'b_k_norm': gain(ks[17], (DEPTH, B_QK)),
        'w_pa': nrm(ks[18], (DEPTH, A_WIDTH, D), A_WIDTH ** -0.5),
        'w_pb': nrm(ks[19], (DEPTH, B_WIDTH, D), B_WIDTH ** -0.5),
        'w_out': nrm(ks[20], (DEPTH, D, D), D ** -0.5),
        'peer_wq': nrm(ks[21], (DEPTH, D, P_HEADS * P_KEY_DIM), D ** -0.5),
        'peer_k1': nrm(ks[22], (DEPTH, N_KEYS, P_HALF), P_HALF ** -0.5),
        'peer_k2': nrm(ks[23], (DEPTH, N_KEYS, P_HALF), P_HALF ** -0.5),
        'peer_u': nrm(ks[24], (DEPTH, N_EXPERTS, D), D ** -0.5),
        'peer_v': nrm(ks[25], (DEPTH, N_EXPERTS, D), P_HEADS ** -0.5),
    }


def reference(x_prompt, x_sample, c_prompt, c_sample, ada_w, ada_b, norm1_g, norm2_g, w_in, a_q_norm, a_k_norm, a_sink, cq_norm, ckv_norm, w_uq, w_ukv, b_q_norm, b_k_norm, w_pa, w_pb, w_out, peer_wq, peer_k1, peer_k2, peer_u, peer_v):
    weights = (ada_w, ada_b, norm1_g, norm2_g, w_in, a_q_norm, a_k_norm, a_sink, cq_norm, ckv_norm, w_uq, w_ukv, b_q_norm, b_k_norm, w_pa, w_pb, w_out, peer_wq, peer_k1, peer_k2, peer_u, peer_v)
    y_prompt = _trunk(x_prompt, c_prompt, *weights)
    y_sample = _trunk(x_sample, c_sample, *weights)
    return (y_prompt, y_sample)
```

```python
import functools
import math

import jax
import jax.numpy as jnp
from jax import lax
from jax.experimental import pallas as pl
from jax.experimental.pallas import tpu as pltpu

F32 = jnp.float32
BF16 = jnp.bfloat16

D = 1024
DEPTH = 2
EPS = 1e-6
NEG_INF = -1e30
LANES = 128
BLOCK = 128
A_HEADS, A_KV, A_DH = 8, 2, 64
WINDOW = 128
B_HEADS, B_NOPE, B_ROPE, B_V = 8, 64, 32, 64
B_QK = B_NOPE + B_ROPE
Q_LORA = KV_LORA = 256
ROPE_BASE = 10000.0
P_HEADS, P_HALF, N_KEYS, P_TOPK = 8, 128, 128, 16
N_EXPERTS = N_KEYS * N_KEYS

C_QA, C_KA, C_VA, C_CQ, C_CKV, C_GA, C_GB, C_KR, N_IN = 0, 512, 768, 1024, 1280, 1536, 2560, 3584, 3712

VMEM_LIMIT = 56 * 1024 * 1024

TM_IN = 512
TM_MIX = 256
TQ_B = 512
TK_B = 512
TL_ROUTE = 128
TM_PEER = 512
TE_PEER = 1024


def _dot(a, b):
    return jnp.dot(a, b, preferred_element_type=F32)


def _dot_nt(a, b):
    return lax.dot_general(a, b, (((1,), (1,)), ((), ())), preferred_element_type=F32)


def _params(sem):
    return pltpu.CompilerParams(dimension_semantics=sem, vmem_limit_bytes=VMEM_LIMIT)


def _ada_kernel(c_ref, w_ref, b_ref, o_ref):
    c = c_ref[...]
    s = c * jax.nn.sigmoid(c)
    w = w_ref[0]
    s_hi = s.astype(BF16)
    s_lo = (s - s_hi.astype(F32)).astype(BF16)
    w_hi = w.astype(BF16)
    w_lo = (w - w_hi.astype(F32)).astype(BF16)
    o_ref[0] = _dot(s_hi, w_hi) + _dot(s_lo, w_hi) + _dot(s_hi, w_lo) + b_ref[0]


def _ada(c_all, ada_w, ada_b):
    n = c_all.shape[0]
    tn = 1536
    return pl.pallas_call(
        _ada_kernel,
        grid=(DEPTH, 6 * D // tn),
        in_specs=[
            pl.BlockSpec((n, D), lambda l, j: (0, 0)),
            pl.BlockSpec((1, D, tn), lambda l, j: (l, 0, j)),
            pl.BlockSpec((1, 1, tn), lambda l, j: (l, 0, j)),
        ],
        out_specs=pl.BlockSpec((1, n, tn), lambda l, j: (l, 0, j)),
        out_shape=jax.ShapeDtypeStruct((DEPTH, n, 6 * D), F32),
        compiler_params=_params(("arbitrary", "arbitrary")),
        name="ada",
    )(c_all, ada_w, ada_b.reshape(DEPTH, 1, 6 * D))


def _rope_tile(x, cos, sin_lo, sin_hi):
    return x * cos + pltpu.roll(x, LANES - B_ROPE // 2, 1) * sin_lo + pltpu.roll(x, B_ROPE // 2, 1) * sin_hi


def _inproj_kernel(x_ref, mod_ref, g1_ref, win_ref, aqg_ref, akg_ref, onesq_ref, onesk_ref,
                   cqg_ref, ckvg_ref, wuq_ref, wuk_ref, wuv_ref, bqg_ref, bkg_ref,
                   cos_ref, sinlo_ref, sinhi_ref,
                   qa_ref, ka_ref, va_ref, qb_ref, kb_ref, vb_ref, ga_ref, gb_ref):
    x = x_ref[0]
    sh1 = mod_ref[0, :, 0:D]
    sc1 = mod_ref[0, :, D:2 * D]
    ms = jnp.mean(x * x, axis=-1, keepdims=True)
    h = (x * lax.rsqrt(ms + EPS)) * g1_ref[...]
    h = h * (1.0 + sc1) + sh1
    hb = h.astype(BF16)

    def proj(a, b):
        return _dot(hb, win_ref[:, a:b])

    z = proj(C_QA, C_KA)
    ss = _dot((z * z).astype(BF16), onesq_ref[...])
    qa_ref[0] = (z * lax.rsqrt(ss * (1.0 / A_DH) + EPS) * aqg_ref[...] * (A_DH ** -0.5)).astype(BF16)
    z = proj(C_KA, C_VA)
    ss = _dot((z * z).astype(BF16), onesk_ref[...])
    ka_ref[0] = (z * lax.rsqrt(ss * (1.0 / A_DH) + EPS) * akg_ref[...]).astype(BF16)
    va_ref[0] = proj(C_VA, C_CQ).astype(BF16)

    cos = cos_ref[...]
    sin_lo = sinlo_ref[...]
    sin_hi = sinhi_ref[...]

    z = proj(C_CQ, C_CKV)
    ms = jnp.mean(z * z, axis=-1, keepdims=True)
    cqn = (z * lax.rsqrt(ms + EPS) * cqg_ref[...]).astype(BF16)
    q = _dot(cqn, wuq_ref[...])
    bqg = bqg_ref[...]
    for hd in range(B_HEADS):
        qh = q[:, hd * LANES:(hd + 1) * LANES]
        ssh = jnp.sum(qh * qh, axis=-1, keepdims=True) * (1.0 / B_QK)
        qh = qh * lax.rsqrt(ssh + EPS) * bqg
        qh = _rope_tile(qh, cos, sin_lo, sin_hi) * (B_QK ** -0.5)
        qb_ref[0, :, hd * LANES:(hd + 1) * LANES] = qh.astype(BF16)

    z = proj(C_CKV, C_GA)
    ms = jnp.mean(z * z, axis=-1, keepdims=True)
    ckvn = (z * lax.rsqrt(ms + EPS) * ckvg_ref[...]).astype(BF16)
    kn = _dot(ckvn, wuk_ref[...])
    vb_ref[0] = _dot(ckvn, wuv_ref[...]).astype(BF16)
    kr = proj(C_KR, N_IN)
    bkg = bkg_ref[...]
    for hd in range(B_HEADS):
        kh = kn[:, hd * LANES:(hd + 1) * LANES] + kr
        ssh = jnp.sum(kh * kh, axis=-1, keepdims=True) * (1.0 / B_QK)
        kh = kh * lax.rsqrt(ssh + EPS) * bkg
        kh = _rope_tile(kh, cos, sin_lo, sin_hi)
        kb_ref[0, :, hd * LANES:(hd + 1) * LANES] = kh.astype(BF16)

    ga_ref[0] = jax.nn.sigmoid(proj(C_GA, C_GB)).astype(BF16)
    gb_ref[0] = jax.nn.sigmoid(proj(C_GB, C_KR)).astype(BF16)


def _inproj(x, mod, lw, rope):
    B, S, _ = x.shape
    tm = TM_IN
    tok = lambda w: pl.BlockSpec((1, tm, w), lambda b, i: (b, i, 0))
    full = lambda a: pl.BlockSpec(a.shape, lambda b, i: (0,) * a.ndim)
    rope_spec = pl.BlockSpec((tm, LANES), lambda b, i: (i, 0))
    consts = [lw["norm1_g"], lw["w_in"], lw["a_qg"], lw["a_kg"], lw["ones_q"], lw["ones_k"],
              lw["cq_g"], lw["ckv_g"], lw["w_uq"], lw["w_uk"], lw["w_uv"], lw["b_qg"], lw["b_kg"]]
    widths = [512, 256, 256, 1024, 1024, 512, 1024, 1024]
    return pl.pallas_call(
        _inproj_kernel,
        grid=(B, S // tm),
        in_specs=[tok(D), pl.BlockSpec((1, 1, 6 * D), lambda b, i: (b, 0, 0))]
                 + [full(a) for a in consts] + [rope_spec] * 3,
        out_specs=[tok(w) for w in widths],
        out_shape=[jax.ShapeDtypeStruct((B, S, w), BF16) for w in widths],
        compiler_params=_params(("parallel", "parallel")),
        name="inproj",
    )(x, mod, *consts, *rope)


def _attn_a_kernel(sink_ref, q_ref, k_ref, v_ref, o_ref):
    n = pl.program_id(1)
    S = k_ref.shape[1]
    span = 3 * BLOCK
    group = A_HEADS // A_KV
    start = pl.multiple_of(jnp.clip((n - 1) * BLOCK, 0, S - span), BLOCK)
    kw = k_ref[0, pl.ds(start, span), :]
    vw = v_ref[0, pl.ds(start, span), :]
    rows = group * BLOCK
    lane = lax.broadcasted_iota(jnp.int32, (BLOCK, LANES), 1)
    qpos = n * BLOCK + lax.broadcasted_iota(jnp.int32, (rows, span), 0) % BLOCK
    kpos = start + lax.broadcasted_iota(jnp.int32, (rows, span), 1)
    dist = jnp.abs(qpos - kpos)
    valid = dist <= WINDOW
    distf = dist.astype(F32)
    rowhead = lax.broadcasted_iota(jnp.int32, (rows, 1), 0) // BLOCK
    for j in range(A_KV):
        kd = kw[:, j * LANES:(j + 1) * LANES]
        vd = vw[:, j * LANES:(j + 1) * LANES]
        parts = []
        for pp in range(group // 2):
            p = (group // 2) * j + pp
            qp = q_ref[0, :, p * LANES:(p + 1) * LANES]
            zero = jnp.zeros_like(qp)
            parts.append(jnp.where(lane < A_DH, qp, zero))
            parts.append(jnp.where(lane >= A_DH, qp, zero))
        q4 = jnp.concatenate(parts, axis=0)
        s = _dot_nt(q4, kd)
        head = (group * j + rowhead + 1).astype(F32)
        slope = jnp.exp2(-8.0 * head / A_HEADS)
        sink = jnp.zeros((rows, 1), F32)
        for g in range(group):
            sink = jnp.where(rowhead == g, sink_ref[group * j + g], sink)
        logits = jnp.where(valid, s - slope * distf, NEG_INF)
        m = jnp.maximum(jnp.max(logits, axis=-1, keepdims=True), sink)
        p = jnp.exp(logits - m)
        denom = jnp.sum(p, axis=-1, keepdims=True) + jnp.exp(sink - m)
        o = _dot((p / denom).astype(BF16), vd)
        for pp in range(group // 2):
            oe = o[(2 * pp) * BLOCK:(2 * pp + 1) * BLOCK]
            oo = o[(2 * pp + 1) * BLOCK:(2 * pp + 2) * BLOCK]
            p_idx = (group // 2) * j + pp
            o_ref[0, :, p_idx * LANES:(p_idx + 1) * LANES] = jnp.where(lane < A_DH, oe, oo).astype(BF16)


def _attn_a(qa, ka, va, sink):
    B, S, _ = qa.shape
    return pl.pallas_call(
        _attn_a_kernel,
        grid=(B, S // BLOCK),
        in_specs=[
            pl.BlockSpec(memory_space=pltpu.SMEM),
            pl.BlockSpec((1, BLOCK, 512), lambda b, n: (b, n, 0)),
            pl.BlockSpec((1, S, 256), lambda b, n: (b, 0, 0)),
            pl.BlockSpec((1, S, 256), lambda b, n: (b, 0, 0)),
        ],
        out_specs=pl.BlockSpec((1, BLOCK, 512), lambda b, n: (b, n, 0)),
        out_shape=jax.ShapeDtypeStruct((B, S, 512), BF16),
        compiler_params=_params(("parallel", "arbitrary")),
        name="attn_a",
    )(sink, qa, ka, va)


def _attn_b_kernel(q_ref, k_ref, v_ref, o_ref):
    S = k_ref.shape[1]
    tq = q_ref.shape[1]
    nk = S // TK_B
    lane = lax.broadcasted_iota(jnp.int32, (tq, LANES), 1)
    outs = []
    for hh in range(2):
        q = q_ref[0, :, hh * LANES:(hh + 1) * LANES]

        def body(j, carry, hh=hh, q=q):
            m, l, acc = carry
            off = pl.multiple_of(j * TK_B, TK_B)
            k = k_ref[0, pl.ds(off, TK_B), hh * LANES:(hh + 1) * LANES]
            v = v_ref[0, pl.ds(off, TK_B), :]
            s = _dot_nt(q, k)
            m_new = jnp.maximum(m, jnp.max(s, axis=-1, keepdims=True))
            alpha = jnp.exp(m - m_new)
            p = jnp.exp(s - m_new)
            l = alpha * l + jnp.sum(p, axis=-1, keepdims=True)
            acc = alpha * acc + _dot(p.astype(BF16), v)
            return m_new, l, acc

        init = (jnp.full((tq, 1), -jnp.inf, F32), jnp.zeros((tq, 1), F32), jnp.zeros((tq, LANES), F32))
        m, l, acc = lax.fori_loop(0, nk, body, init)
        outs.append(acc / l)
    o_ref[0] = jnp.where(lane < B_V, outs[0], outs[1]).astype(BF16)


def _attn_b(qb, kb, vb):
    B, S, _ = qb.shape
    tq = TQ_B
    return pl.pallas_call(
        _attn_b_kernel,
        grid=(B, B_HEADS // 2, S // tq),
        in_specs=[
            pl.BlockSpec((1, tq, 2 * LANES), lambda b, p, i: (b, i, p)),
            pl.BlockSpec((1, S, 2 * LANES), lambda b, p, i: (b, 0, p)),
            pl.BlockSpec((1, S, LANES), lambda b, p, i: (b, 0, p)),
        ],
        out_specs=pl.BlockSpec((1, tq, LANES), lambda b, p, i: (b, i, p)),
        out_shape=jax.ShapeDtypeStruct((B, S, B_HEADS * B_V), BF16),
        compiler_params=_params(("parallel", "parallel", "arbitrary")),
        name="attn_b",
    )(qb, kb, vb)


def _mixout_kernel(x_ref, ya_ref, yb_ref, ga_ref, gb_ref, mod_ref, wpa_ref, wpb_ref, wout_ref,
                   g2_ref, wq_ref, k1_ref, k2_ref, x1_ref, h2_ref, sc_ref):
    ma = _dot(ya_ref[0], wpa_ref[...])
    mb = _dot(yb_ref[0], wpb_ref[...])
    m = ga_ref[0].astype(F32) * ma + gb_ref[0].astype(F32) * mb
    mo = _dot(m.astype(BF16), wout_ref[...])
    g1 = mod_ref[0, :, 2 * D:3 * D]
    sh2 = mod_ref[0, :, 3 * D:4 * D]
    sc2 = mod_ref[0, :, 4 * D:5 * D]
    x1 = x_ref[0] + g1 * mo
    x1_ref[0] = x1
    ms = jnp.mean(x1 * x1, axis=-1, keepdims=True)
    h2 = (x1 * lax.rsqrt(ms + EPS)) * g2_ref[...]
    h2 = (h2 * (1.0 + sc2) + sh2).astype(BF16)
    h2_ref[0] = h2
    for hd in range(P_HEADS):
        for c, kref in enumerate((k1_ref, k2_ref)):
            col = (2 * hd + c) * P_HALF
            qhc = _dot(h2, wq_ref[:, col:col + P_HALF]).astype(BF16)
            sc_ref[0, 2 * hd + c] = _dot_nt(kref[...], qhc)


def _mixout(x, ya, yb, ga, gb, mod, lw):
    B, S, _ = x.shape
    tm = TM_MIX
    tok = lambda w: pl.BlockSpec((1, tm, w), lambda b, i: (b, i, 0))
    full = lambda a: pl.BlockSpec(a.shape, lambda b, i: (0,) * a.ndim)
    consts = [lw["w_pa"], lw["w_pb"], lw["w_out"], lw["norm2_g"], lw["peer_wq"], lw["peer_k1"], lw["peer_k2"]]
    return pl.pallas_call(
        _mixout_kernel,
        grid=(B, S // tm),
        in_specs=[tok(D), tok(512), tok(512), tok(D), tok(D),
                  pl.BlockSpec((1, 1, 6 * D), lambda b, i: (b, 0, 0))] + [full(a) for a in consts],
        out_specs=[tok(D), tok(D), pl.BlockSpec((1, 2 * P_HEADS, N_KEYS, tm), lambda b, i: (b, 0, 0, i))],
        out_shape=[jax.ShapeDtypeStruct((B, S, D), F32), jax.ShapeDtypeStruct((B, S, D), BF16),
                   jax.ShapeDtypeStruct((B, 2 * P_HEADS, N_KEYS, S), F32)],
        compiler_params=_params(("parallel", "parallel")),
        name="mixout",
    )(x, ya, yb, ga, gb, mod, *consts)


def _extract_topk(s, key_iota):
    rank = jnp.full(s.shape, float(P_TOPK), F32)
    vals = []
    for a in range(P_TOPK):
        m = jnp.max(s, axis=0, keepdims=True)
        idx = jnp.min(jnp.where(s == m, key_iota, float(N_KEYS)), axis=0, keepdims=True)
        hit = key_iota == idx
        rank = jnp.where(hit, float(a), rank)
        s = jnp.where(hit, -jnp.inf, s)
        vals.append(m)
    return vals, rank


def _route_kernel(sc_ref, e1_ref, nb_ref, e2_ref, r2_ref):
    tl = sc_ref.shape[3]
    key_iota = lax.broadcasted_iota(jnp.int32, (N_KEYS, tl), 0).astype(F32)
    b_iota = lax.broadcasted_iota(jnp.int32, (P_TOPK, tl), 0).astype(F32)

    def head(hd, carry):
        s1 = sc_ref[0, 2 * hd]
        s2 = sc_ref[0, 2 * hd + 1]
        v1, rank1 = _extract_topk(s1, key_iota)
        v2, rank2 = _extract_topk(s2, key_iota)
        v2all = jnp.concatenate(v2, axis=0)
        cand = [v1[a] + v2all for a in range(P_TOPK)]
        flat = [b_iota + float(a * P_TOPK) for a in range(P_TOPK)]
        big = float(P_TOPK * P_TOPK)
        m0 = None
        z = jnp.zeros((1, tl), F32)
        for k in range(P_TOPK):
            m = cand[0]
            for a in range(1, P_TOPK):
                m = jnp.maximum(m, cand[a])
            m = jnp.max(m, axis=0, keepdims=True)
            idx = None
            for a in range(P_TOPK):
                cur = jnp.where(cand[a] == m, flat[a], big)
                idx = cur if idx is None else jnp.minimum(idx, cur)
            idx = jnp.min(idx, axis=0, keepdims=True)
            cand = [jnp.where(flat[a] == idx, -jnp.inf, cand[a]) for a in range(P_TOPK)]
            if m0 is None:
                m0 = m
            z = z + jnp.exp(m - m0)
        nb1 = jnp.zeros((N_KEYS, tl), F32)
        for a in range(P_TOPK):
            nb_a = jnp.sum(jnp.where(cand[a] == -jnp.inf, 1.0, 0.0), axis=0, keepdims=True)
            nb1 = jnp.where(rank1 == float(a), nb_a, nb1)
        e1_ref[0, hd] = jnp.exp(s1 - v1[0]) / z
        nb_ref[0, hd] = nb1
        e2_ref[0, hd] = jnp.exp(s2 - v2[0])
        r2_ref[0, hd] = rank2
        return carry

    lax.fori_loop(0, P_HEADS, head, 0)


def _route(scores):
    B, _, _, S = scores.shape
    tl = TL_ROUTE
    spec = pl.BlockSpec((1, P_HEADS, N_KEYS, tl), lambda b, i: (b, 0, 0, i))
    shp = jax.ShapeDtypeStruct((B, P_HEADS, N_KEYS, S), F32)
    return pl.pallas_call(
        _route_kernel,
        grid=(B, S // tl),
        in_specs=[pl.BlockSpec((1, 2 * P_HEADS, N_KEYS, tl), lambda b, i: (b, 0, 0, i))],
        out_specs=[spec] * 4,
        out_shape=[shp] * 4,
        compiler_params=_params(("parallel", "parallel")),
        name="route",
    )(scores)


def _peer_kernel(h_ref, e1_ref, nb_ref, e2_ref, r2_ref, u_ref, vt_ref, x1_ref, mod_ref, y_ref, acc_ref):
    e = pl.program_id(2)
    rows_per_step = TE_PEER // N_KEYS

    @pl.when(e == 0)
    def _():
        acc_ref[...] = jnp.zeros_like(acc_ref)

    a = _dot_nt(u_ref[...], h_ref[0])
    gl = 0.5 * a * (1.0 + lax.erf(a * (1.0 / math.sqrt(2.0))))
    parts = []
    for r in range(rows_per_step):
        w = None
        for hd in range(P_HEADS):
            nb = nb_ref[0, hd, r:r + 1, :]
            e1 = e1_ref[0, hd, r:r + 1, :]
            t = jnp.where(r2_ref[0, hd] < nb, e2_ref[0, hd], 0.0) * e1
            w = t if w is None else w + t
        parts.append((w * gl[r * N_KEYS:(r + 1) * N_KEYS]).astype(BF16))
    mt = jnp.concatenate(parts, axis=0)
    acc_ref[...] += _dot(vt_ref[...], mt)

    @pl.when(e == pl.num_programs(2) - 1)
    def _():
        g2 = mod_ref[0, :, 5 * D:6 * D]
        y_ref[0] = x1_ref[0] + g2 * acc_ref[...].T


def _peer(h2, e1, nb, e2, r2, x1, mod, lw):
    B, S, _ = h2.shape
    tm, te = TM_PEER, TE_PEER
    rows = te // N_KEYS
    tok = pl.BlockSpec((1, tm, D), lambda b, i, e: (b, i, 0))
    slab = pl.BlockSpec((1, P_HEADS, rows, tm), lambda b, i, e: (b, 0, e, i))
    dense = pl.BlockSpec((1, P_HEADS, N_KEYS, tm), lambda b, i, e: (b, 0, 0, i))
    return pl.pallas_call(
        _peer_kernel,
        grid=(B, S // tm, N_EXPERTS // te),
        in_specs=[tok, slab, slab, dense, dense,
                  pl.BlockSpec((te, D), lambda b, i, e: (e, 0)),
                  pl.BlockSpec((D, te), lambda b, i, e: (0, e)),
                  tok, pl.BlockSpec((1, 1, 6 * D), lambda b, i, e: (b, 0, 0))],
        out_specs=tok,
        out_shape=jax.ShapeDtypeStruct((B, S, D), F32),
        scratch_shapes=[pltpu.VMEM((D, tm), F32)],
        compiler_params=_params(("parallel", "parallel", "arbitrary")),
        name="peer",
    )(h2, e1, nb, e2, r2, lw["peer_u"], lw["peer_vt"], x1, mod)


def _block_ones(n, blk):
    i = jnp.arange(n) // blk
    return (i[:, None] == i[None, :]).astype(BF16)


def _pad_cols(w, n):
    return jnp.concatenate([w, jnp.zeros((w.shape[0], n - w.shape[1]), w.dtype)], axis=1)


def _layer_weights(l, norm1_g, norm2_g, w_in, a_q_norm, a_k_norm, cq_norm, ckv_norm, w_uq, w_ukv,
                   b_q_norm, b_k_norm, w_pa, w_pb, w_out, peer_wq, peer_k1, peer_k2, peer_u, peer_v):
    w = w_in[l]
    k0, k1 = w[:, 512:576], w[:, 576:640]
    v0, v1 = w[:, 640:704], w[:, 704:768]
    z64 = jnp.zeros((D, B_NOPE), F32)
    z32 = jnp.zeros((D, LANES - B_QK), F32)
    w_in_r = jnp.concatenate([w[:, 0:512], k0, k0, k1, k1, v0, v0, v1, v1, w[:, 768:1024], w[:, 1024:1280],
                              w[:, 1312:2336], w[:, 2336:3360], z64, w[:, 1280:1312], z32], axis=1).astype(BF16)
    uq = w_uq[l].reshape(Q_LORA, B_HEADS, B_QK)
    w_uq_r = jnp.pad(uq, ((0, 0), (0, 0), (0, LANES - B_QK))).reshape(Q_LORA, B_HEADS * LANES).astype(BF16)
    ukv = w_ukv[l].reshape(KV_LORA, B_HEADS, B_NOPE + B_V)
    w_uk_r = jnp.pad(ukv[:, :, :B_NOPE], ((0, 0), (0, 0), (0, LANES - B_NOPE))).reshape(KV_LORA, B_HEADS * LANES).astype(BF16)
    w_uv_r = ukv[:, :, B_NOPE:].reshape(KV_LORA, B_HEADS * B_V).astype(BF16)
    row = lambda a: a.reshape(1, -1)
    return {
        "norm1_g": row(norm1_g[l]), "norm2_g": row(norm2_g[l]), "w_in": w_in_r,
        "a_qg": row(jnp.tile(a_q_norm[l], A_HEADS)), "a_kg": row(jnp.tile(a_k_norm[l], 2 * A_KV)),
        "ones_q": _block_ones(A_HEADS * A_DH, A_DH), "ones_k": _block_ones(2 * A_KV * A_DH, A_DH),
        "cq_g": row(cq_norm[l]), "ckv_g": row(ckv_norm[l]),
        "w_uq": w_uq_r, "w_uk": w_uk_r, "w_uv": w_uv_r,
        "b_qg": row(_pad_cols(b_q_norm[l][None], LANES)), "b_kg": row(_pad_cols(b_k_norm[l][None], LANES)),
        "w_pa": w_pa[l].astype(BF16), "w_pb": w_pb[l].astype(BF16), "w_out": w_out[l].astype(BF16),
        "peer_wq": peer_wq[l].astype(BF16), "peer_k1": peer_k1[l].astype(BF16), "peer_k2": peer_k2[l].astype(BF16),
        "peer_u": peer_u[l].astype(BF16), "peer_vt": peer_v[l].astype(BF16).T,
    }


def _rope_tables(S):
    half = B_ROPE // 2
    inv = jnp.power(jnp.float32(ROPE_BASE), -jnp.arange(half, dtype=F32) / half)
    ang = jnp.arange(S, dtype=F32)[:, None] * inv[None, :]
    cos, sin = jnp.cos(ang), jnp.sin(ang)
    zeros = jnp.zeros((S, half), F32)
    pad = jnp.zeros((S, LANES - B_QK), F32)
    cos_t = jnp.concatenate([jnp.ones((S, B_NOPE), F32), cos, cos, pad + 1.0], axis=1)
    sin_lo = jnp.concatenate([jnp.zeros((S, B_NOPE), F32), -sin, zeros, pad], axis=1)
    sin_hi = jnp.concatenate([jnp.zeros((S, B_NOPE), F32), zeros, sin, pad], axis=1)
    return cos_t, sin_lo, sin_hi


def _trunk(x, mods, layers, sinks):
    rope = _rope_tables(x.shape[1])
    for l in range(DEPTH):
        lw = layers[l]
        mod = mods[l]
        qa, ka, va, qb, kb, vb, ga, gb = _inproj(x, mod, lw, rope)
        ya = _attn_a(qa, ka, va, sinks[l])
        yb = _attn_b(qb, kb, vb)
        x1, h2, scores = _mixout(x, ya, yb, ga, gb, mod, lw)
        e1, nb, e2, r2 = _route(scores)
        x = _peer(h2, e1, nb, e2, r2, x1, mod, lw)
    return x


def kernel(x_prompt, x_sample, c_prompt, c_sample, ada_w, ada_b, norm1_g, norm2_g, w_in, a_q_norm, a_k_norm, a_sink, cq_norm, ckv_norm, w_uq, w_ukv, b_q_norm, b_k_norm, w_pa, w_pb, w_out, peer_wq, peer_k1, peer_k2, peer_u, peer_v):
    nb_prompt = c_prompt.shape[0]
    mod_all = _ada(jnp.concatenate([c_prompt, c_sample], axis=0), ada_w, ada_b)
    mods_p = [mod_all[l, :nb_prompt].reshape(nb_prompt, 1, 6 * D) for l in range(DEPTH)]
    mods_s = [mod_all[l, nb_prompt:].reshape(-1, 1, 6 * D) for l in range(DEPTH)]
    layers = [_layer_weights(l, norm1_g, norm2_g, w_in, a_q_norm, a_k_norm, cq_norm, ckv_norm, w_uq, w_ukv,
                             b_q_norm, b_k_norm, w_pa, w_pb, w_out, peer_wq, peer_k1, peer_k2, peer_u, peer_v)
              for l in range(DEPTH)]
    sinks = [a_sink[l] for l in range(DEPTH)]
    y_prompt = _trunk(x_prompt, mods_p, layers, sinks)
    y_sample = _trunk(x_sample, mods_s, layers, sinks)
    return (y_prompt, y_sample)
```

```python
import functools
import math

import jax
import jax.numpy as jnp
from jax import lax
from jax.experimental import pallas as pl
from jax.experimental.pallas import tpu as pltpu

F32 = jnp.float32
BF16 = jnp.bfloat16

D = 1024
DEPTH = 2
EPS = 1e-6
NEG_INF = -1e30
LANES = 128
BF16_ROWS = 16
BLOCK = 128
A_HEADS, A_KV, A_DH = 8, 2, 64
WINDOW = 128
B_HEADS, B_NOPE, B_ROPE, B_V = 8, 64, 32, 64
B_QK = B_NOPE + B_ROPE
Q_LORA = KV_LORA = 256
ROPE_BASE = 10000.0
P_HEADS, P_HALF, N_KEYS, P_TOPK = 8, 128, 128, 16
N_EXPERTS = N_KEYS * N_KEYS

C_QA, C_KA, C_VA, C_CQ, C_CKV, C_GA, C_GB, C_KR, N_IN = 0, 512, 768, 1024, 1280, 1536, 2560, 3584, 3712

VMEM_LIMIT = 56 * 1024 * 1024

TM_IN = 512
TM_MIX = 256
TQ_B = 512
TK_B = 512
TL_ROUTE = 128
TM_PEER = 512
TE_PEER = 1024


def _dot(a, b):
    return jnp.dot(a, b, preferred_element_type=F32)


def _dot_nt(a, b):
    return lax.dot_general(a, b, (((1,), (1,)), ((), ())), preferred_element_type=F32)


def _params(sem):
    return pltpu.CompilerParams(dimension_semantics=sem, vmem_limit_bytes=VMEM_LIMIT)


def _ada_kernel(c_ref, w_ref, b_ref, o_ref):
    c = c_ref[...]
    s = c * jax.nn.sigmoid(c)
    w = w_ref[0]
    s_hi = s.astype(BF16)
    s_lo = (s - s_hi.astype(F32)).astype(BF16)
    w_hi = w.astype(BF16)
    w_lo = (w - w_hi.astype(F32)).astype(BF16)
    o_ref[0] = _dot(s_hi, w_hi) + _dot(s_lo, w_hi) + _dot(s_hi, w_lo) + b_ref[0]


def _ada(c_all, ada_w, ada_b):
    n = c_all.shape[0]
    tn = 1536
    return pl.pallas_call(
        _ada_kernel,
        grid=(DEPTH, 6 * D // tn),
        in_specs=[
            pl.BlockSpec((n, D), lambda l, j: (0, 0)),
            pl.BlockSpec((1, D, tn), lambda l, j: (l, 0, j)),
            pl.BlockSpec((1, 1, tn), lambda l, j: (l, 0, j)),
        ],
        out_specs=pl.BlockSpec((1, n, tn), lambda l, j: (l, 0, j)),
        out_shape=jax.ShapeDtypeStruct((DEPTH, n, 6 * D), F32),
        compiler_params=_params(("arbitrary", "arbitrary")),
        name="ada",
    )(c_all, ada_w, ada_b.reshape(DEPTH, 1, 6 * D))


def _rope_tile(x, cos, sin_lo, sin_hi):
    return x * cos + pltpu.roll(x, LANES - B_ROPE // 2, 1) * sin_lo + pltpu.roll(x, B_ROPE // 2, 1) * sin_hi


def _inproj_kernel(x_ref, mod_ref, g1_ref, win_ref, aqg_ref, akg_ref, onesq_ref, onesk_ref,
                   cqg_ref, ckvg_ref, wuq_ref, wuk_ref, wuv_ref, bqg_ref, bkg_ref,
                   cos_ref, sinlo_ref, sinhi_ref,
                   qa_ref, ka_ref, va_ref, qb_ref, kb_ref, vb_ref, ga_ref, gb_ref):
    x = x_ref[0]
    sh1 = mod_ref[0, :, 0:D]
    sc1 = mod_ref[0, :, D:2 * D]
    ms = jnp.mean(x * x, axis=-1, keepdims=True)
    h = (x * lax.rsqrt(ms + EPS)) * g1_ref[...]
    h = h * (1.0 + sc1) + sh1
    hb = h.astype(BF16)

    def proj(a, b):
        return _dot(hb, win_ref[:, a:b])

    z = proj(C_QA, C_KA)
    ss = _dot((z * z).astype(BF16), onesq_ref[...])
    qa_ref[0] = (z * lax.rsqrt(ss * (1.0 / A_DH) + EPS) * aqg_ref[...] * (A_DH ** -0.5)).astype(BF16)
    z = proj(C_KA, C_VA)
    ss = _dot((z * z).astype(BF16), onesk_ref[...])
    ka_ref[0] = (z * lax.rsqrt(ss * (1.0 / A_DH) + EPS) * akg_ref[...]).astype(BF16)
    va_ref[0] = proj(C_VA, C_CQ).astype(BF16)

    cos = cos_ref[...]
    sin_lo = sinlo_ref[...]
    sin_hi = sinhi_ref[...]

    z = proj(C_CQ, C_CKV)
    ms = jnp.mean(z * z, axis=-1, keepdims=True)
    cqn = (z * lax.rsqrt(ms + EPS) * cqg_ref[...]).astype(BF16)
    q = _dot(cqn, wuq_ref[...])
    bqg = bqg_ref[...]
    for hd in range(B_HEADS):
        qh = q[:, hd * LANES:(hd + 1) * LANES]
        ssh = jnp.sum(qh * qh, axis=-1, keepdims=True) * (1.0 / B_QK)
        qh = qh * lax.rsqrt(ssh + EPS) * bqg
        qh = _rope_tile(qh, cos, sin_lo, sin_hi) * (B_QK ** -0.5)
        qb_ref[0, :, hd * LANES:(hd + 1) * LANES] = qh.astype(BF16)

    z = proj(C_CKV, C_GA)
    ms = jnp.mean(z * z, axis=-1, keepdims=True)
    ckvn = (z * lax.rsqrt(ms + EPS) * ckvg_ref[...]).astype(BF16)
    kn = _dot(ckvn, wuk_ref[...])
    vb_ref[0] = _dot(ckvn, wuv_ref[...]).astype(BF16)
    kr = proj(C_KR, N_IN)
    bkg = bkg_ref[...]
    for hd in range(B_HEADS):
        kh = kn[:, hd * LANES:(hd + 1) * LANES] + kr
        ssh = jnp.sum(kh * kh, axis=-1, keepdims=True) * (1.0 / B_QK)
        kh = kh * lax.rsqrt(ssh + EPS) * bkg
        kh = _rope_tile(kh, cos, sin_lo, sin_hi)
        kb_ref[0, :, hd * LANES:(hd + 1) * LANES] = kh.astype(BF16)

    ga_ref[0] = jax.nn.sigmoid(proj(C_GA, C_GB)).astype(BF16)
    gb_ref[0] = jax.nn.sigmoid(proj(C_GB, C_KR)).astype(BF16)


def _inproj(x, mod, lw, rope):
    B, S, _ = x.shape
    tm = TM_IN
    tok = lambda w: pl.BlockSpec((1, tm, w), lambda b, i: (b, i, 0))
    full = lambda a: pl.BlockSpec(a.shape, lambda b, i: (0,) * a.ndim)
    rope_spec = pl.BlockSpec((tm, LANES), lambda b, i: (i, 0))
    consts = [lw["norm1_g"], lw["w_in"], lw["a_qg"], lw["a_kg"], lw["ones_q"], lw["ones_k"],
              lw["cq_g"], lw["ckv_g"], lw["w_uq"], lw["w_uk"], lw["w_uv"], lw["b_qg"], lw["b_kg"]]
    widths = [512, 256, 256, 1024, 1024, 512, 1024, 1024]
    return pl.pallas_call(
        _inproj_kernel,
        grid=(B, S // tm),
        in_specs=[tok(D), pl.BlockSpec((1, 1, 6 * D), lambda b, i: (b, 0, 0))]
                 + [full(a) for a in consts] + [rope_spec] * 3,
        out_specs=[tok(w) for w in widths],
        out_shape=[jax.ShapeDtypeStruct((B, S, w), BF16) for w in widths],
        compiler_params=_params(("parallel", "parallel")),
        name="inproj",
    )(x, mod, *consts, *rope)


def _attn_a_kernel(sink_ref, q_ref, k_ref, v_ref, o_ref):
    n = pl.program_id(1)
    S = k_ref.shape[1]
    span = 3 * BLOCK
    group = A_HEADS // A_KV
    start = pl.multiple_of(jnp.clip((n - 1) * BLOCK, 0, S - span), BLOCK)
    kw = k_ref[0, pl.ds(start, span), :]
    vw = v_ref[0, pl.ds(start, span), :]
    rows = group * BLOCK
    lane = lax.broadcasted_iota(jnp.int32, (BLOCK, LANES), 1)
    qpos = n * BLOCK + lax.broadcasted_iota(jnp.int32, (rows, span), 0) % BLOCK
    kpos = start + lax.broadcasted_iota(jnp.int32, (rows, span), 1)
    dist = jnp.abs(qpos - kpos)
    valid = dist <= WINDOW
    distf = dist.astype(F32)
    rowhead = lax.broadcasted_iota(jnp.int32, (rows, 1), 0) // BLOCK
    for j in range(A_KV):
        kd = kw[:, j * LANES:(j + 1) * LANES]
        vd = vw[:, j * LANES:(j + 1) * LANES]
        parts = []
        for pp in range(group // 2):
            p = (group // 2) * j + pp
            qp = q_ref[0, :, p * LANES:(p + 1) * LANES]
            zero = jnp.zeros_like(qp)
            parts.append(jnp.where(lane < A_DH, qp, zero))
            parts.append(jnp.where(lane >= A_DH, qp, zero))
        q4 = jnp.concatenate(parts, axis=0)
        s = _dot_nt(q4, kd)
        head = (group * j + rowhead + 1).astype(F32)
        slope = jnp.exp2(-8.0 * head / A_HEADS)
        sink = jnp.zeros((rows, 1), F32)
        for g in range(group):
            sink = jnp.where(rowhead == g, sink_ref[group * j + g], sink)
        logits = jnp.where(valid, s - slope * distf, NEG_INF)
        m = jnp.maximum(jnp.max(logits, axis=-1, keepdims=True), sink)
        p = jnp.exp(logits - m)
        denom = jnp.sum(p, axis=-1, keepdims=True) + jnp.exp(sink - m)
        o = _dot((p / denom).astype(BF16), vd)
        for pp in range(group // 2):
            oe = o[(2 * pp) * BLOCK:(2 * pp + 1) * BLOCK]
            oo = o[(2 * pp + 1) * BLOCK:(2 * pp + 2) * BLOCK]
            p_idx = (group // 2) * j + pp
            o_ref[0, :, p_idx * LANES:(p_idx + 1) * LANES] = jnp.where(lane < A_DH, oe, oo).astype(BF16)


def _attn_a(qa, ka, va, sink):
    B, S, _ = qa.shape
    return pl.pallas_call(
        _attn_a_kernel,
        grid=(B, S // BLOCK),
        in_specs=[
            pl.BlockSpec(memory_space=pltpu.SMEM),
            pl.BlockSpec((1, BLOCK, 512), lambda b, n: (b, n, 0)),
            pl.BlockSpec((1, S, 256), lambda b, n: (b, 0, 0)),
            pl.BlockSpec((1, S, 256), lambda b, n: (b, 0, 0)),
        ],
        out_specs=pl.BlockSpec((1, BLOCK, 512), lambda b, n: (b, n, 0)),
        out_shape=jax.ShapeDtypeStruct((B, S, 512), BF16),
        compiler_params=_params(("parallel", "arbitrary")),
        name="attn_a",
    )(sink, qa, ka, va)


def _attn_b_kernel(q_ref, k_ref, v_ref, o_ref):
    S = k_ref.shape[1]
    tq = q_ref.shape[1]
    nk = S // TK_B
    lane = lax.broadcasted_iota(jnp.int32, (tq, LANES), 1)
    outs = []
    for hh in range(2):
        q = q_ref[0, :, hh * LANES:(hh + 1) * LANES]

        def body(j, carry, hh=hh, q=q):
            m, l, acc = carry
            off = pl.multiple_of(j * TK_B, TK_B)
            k = k_ref[0, pl.ds(off, TK_B), hh * LANES:(hh + 1) * LANES]
            v = v_ref[0, pl.ds(off, TK_B), :]
            s = _dot_nt(q, k)
            m_new = jnp.maximum(m, jnp.max(s, axis=-1, keepdims=True))
            alpha = jnp.exp(m - m_new)
            p = jnp.exp(s - m_new)
            l = alpha * l + jnp.sum(p, axis=-1, keepdims=True)
            acc = alpha * acc + _dot(p.astype(BF16), v)
            return m_new, l, acc

        init = (jnp.full((tq, 1), -jnp.inf, F32), jnp.zeros((tq, 1), F32), jnp.zeros((tq, LANES), F32))
        m, l, acc = lax.fori_loop(0, nk, body, init)
        outs.append(acc / l)
    o_ref[0] = jnp.where(lane < B_V, outs[0], outs[1]).astype(BF16)


def _attn_b(qb, kb, vb):
    B, S, _ = qb.shape
    tq = TQ_B
    return pl.pallas_call(
        _attn_b_kernel,
        grid=(B, B_HEADS // 2, S // tq),
        in_specs=[
            pl.BlockSpec((1, tq, 2 * LANES), lambda b, p, i: (b, i, p)),
            pl.BlockSpec((1, S, 2 * LANES), lambda b, p, i: (b, 0, p)),
            pl.BlockSpec((1, S, LANES), lambda b, p, i: (b, 0, p)),
        ],
        out_specs=pl.BlockSpec((1, tq, LANES), lambda b, p, i: (b, i, p)),
        out_shape=jax.ShapeDtypeStruct((B, S, B_HEADS * B_V), BF16),
        compiler_params=_params(("parallel", "parallel", "arbitrary")),
        name="attn_b",
    )(qb, kb, vb)


def _mixout_kernel(x_ref, ya_ref, yb_ref, ga_ref, gb_ref, mod_ref, wpa_ref, wpb_ref, wout_ref,
                   g2_ref, wq_ref, k1_ref, k2_ref, x1_ref, h2_ref, sc_ref):
    ma = _dot(ya_ref[0], wpa_ref[...])
    mb = _dot(yb_ref[0], wpb_ref[...])
    m = ga_ref[0].astype(F32) * ma + gb_ref[0].astype(F32) * mb
    mo = _dot(m.astype(BF16), wout_ref[...])
    g1 = mod_ref[0, :, 2 * D:3 * D]
    sh2 = mod_ref[0, :, 3 * D:4 * D]
    sc2 = mod_ref[0, :, 4 * D:5 * D]
    x1 = x_ref[0] + g1 * mo
    x1_ref[0] = x1
    ms = jnp.mean(x1 * x1, axis=-1, keepdims=True)
    h2 = (x1 * lax.rsqrt(ms + EPS)) * g2_ref[...]
    h2 = (h2 * (1.0 + sc2) + sh2).astype(BF16)
    h2_ref[0] = h2
    for hd in range(P_HEADS):
        for c, kref in enumerate((k1_ref, k2_ref)):
            col = (2 * hd + c) * P_HALF
            qhc = _dot(h2, wq_ref[:, col:col + P_HALF]).astype(BF16)
            sc_ref[0, 2 * hd + c] = _dot_nt(kref[...], qhc)


def _mixout(x, ya, yb, ga, gb, mod, lw):
    B, S, _ = x.shape
    tm = TM_MIX
    tok = lambda w: pl.BlockSpec((1, tm, w), lambda b, i: (b, i, 0))
    full = lambda a: pl.BlockSpec(a.shape, lambda b, i: (0,) * a.ndim)
    consts = [lw["w_pa"], lw["w_pb"], lw["w_out"], lw["norm2_g"], lw["peer_wq"], lw["peer_k1"], lw["peer_k2"]]
    return pl.pallas_call(
        _mixout_kernel,
        grid=(B, S // tm),
        in_specs=[tok(D), tok(512), tok(512), tok(D), tok(D),
                  pl.BlockSpec((1, 1, 6 * D), lambda b, i: (b, 0, 0))] + [full(a) for a in consts],
        out_specs=[tok(D), tok(D), pl.BlockSpec((1, 2 * P_HEADS, N_KEYS, tm), lambda b, i: (b, 0, 0, i))],
        out_shape=[jax.ShapeDtypeStruct((B, S, D), F32), jax.ShapeDtypeStruct((B, S, D), BF16),
                   jax.ShapeDtypeStruct((B, 2 * P_HEADS, N_KEYS, S), F32)],
        compiler_params=_params(("parallel", "parallel")),
        name="mixout",
    )(x, ya, yb, ga, gb, mod, *consts)


def _extract_topk(s, key_iota, break_ties):
    rank = jnp.full(s.shape, float(P_TOPK), F32)
    vals = []
    for a in range(P_TOPK):
        m = jnp.max(s, axis=0, keepdims=True)
        hit = s == m
        if break_ties:
            idx = jnp.min(jnp.where(hit, key_iota, float(N_KEYS)), axis=0, keepdims=True)
            hit = key_iota == idx
        rank = jnp.where(hit, float(a), rank)
        s = jnp.where(hit, -jnp.inf, s)
        vals.append(m)
    return vals, rank


def _count(mask):
    return jnp.sum(jnp.where(mask, 1.0, 0.0), axis=0, keepdims=True)


def _route_head(s1, s2, key_iota, b8, break_ties):
    tl = s1.shape[1]
    v1, rank1 = _extract_topk(s1, key_iota, break_ties)
    v2, rank2 = _extract_topk(s2, key_iota, break_ties)
    v2lo = jnp.concatenate(v2[:8], axis=0)
    v2hi = jnp.concatenate(v2[8:], axis=0)
    v1hi = jnp.concatenate(v1[8:], axis=0)
    tiles = [v1[0] + v2lo, v1[0] + v2hi, v1[1] + v2lo]
    flats = [b8, b8 + 8.0, b8 + float(P_TOPK)]
    valid = [None, None, None]
    for a in range(2, 8):
        ok = b8 < float(P_TOPK // (a + 1))
        tiles.append(jnp.where(ok, v1[a] + v2lo, -jnp.inf))
        flats.append(b8 + float(a * P_TOPK))
        valid.append(ok)
    tiles.append(v1hi + v2[0])
    flats.append((b8 + 8.0) * float(P_TOPK))
    valid.append(None)
    big = float(P_TOPK * P_TOPK)
    m0 = None
    z = jnp.zeros((1, tl), F32)
    for k in range(P_TOPK):
        m = tiles[0]
        for t in tiles[1:]:
            m = jnp.maximum(m, t)
        m = jnp.max(m, axis=0, keepdims=True)
        if break_ties:
            idx = None
            for t, f in zip(tiles, flats):
                cur = jnp.where(t == m, f, big)
                idx = cur if idx is None else jnp.minimum(idx, cur)
            idx = jnp.min(idx, axis=0, keepdims=True)
            tiles = [jnp.where(f == idx, -jnp.inf, t) for t, f in zip(tiles, flats)]
        else:
            tiles = [jnp.where(t == m, -jnp.inf, t) for t in tiles]
        if m0 is None:
            m0 = m
        z = z + jnp.exp(m - m0)
    taken = [t == -jnp.inf if ok is None else (t == -jnp.inf) & ok for t, ok in zip(tiles, valid)]
    nb = [_count(taken[0]) + _count(taken[1]), _count(taken[2])] + [_count(taken[a + 1]) for a in range(2, 8)]
    last = jnp.where(taken[9], 1.0, 0.0)
    nb += [last[a:a + 1] for a in range(8)]
    total = nb[0]
    for x in nb[1:]:
        total = total + x
    nb1 = jnp.zeros((N_KEYS, tl), F32)
    for a in range(P_TOPK):
        nb1 = jnp.where(rank1 == float(a), nb[a], nb1)
    want = float(P_TOPK)
    bad = (_count(rank1 < want) != want) | (_count(rank2 < want) != want) | (total != want)
    e1 = jnp.exp(s1 - v1[0]) / z
    e2 = jnp.exp(s2 - v2[0])
    return e1, nb1, e2, rank2, bad


def _route_kernel(sc_ref, e1_ref, nb_ref, e2_ref, r2_ref):
    tl = sc_ref.shape[3]
    key_iota = lax.broadcasted_iota(jnp.int32, (N_KEYS, tl), 0).astype(F32)
    b8 = lax.broadcasted_iota(jnp.int32, (8, tl), 0).astype(F32)

    def head(hd, carry):
        def run(break_ties):
            e1, nb1, e2, rank2, bad = _route_head(sc_ref[0, 2 * hd], sc_ref[0, 2 * hd + 1], key_iota, b8, break_ties)
            e1_ref[0, hd] = e1
            nb_ref[0, hd] = nb1
            e2_ref[0, hd] = e2.astype(BF16)
            r2_ref[0, hd] = rank2.astype(BF16)
            return bad

        bad = run(False)

        @pl.when(jnp.max(jnp.where(bad, 1.0, 0.0)) > 0.0)
        def _():
            run(True)

        return carry

    lax.fori_loop(0, P_HEADS, head, 0)


def _route(scores):
    B, _, _, S = scores.shape
    tl = TL_ROUTE
    spec = pl.BlockSpec((1, P_HEADS, N_KEYS, tl), lambda b, i: (b, 0, 0, i))
    shp = lambda dt: jax.ShapeDtypeStruct((B, P_HEADS, N_KEYS, S), dt)
    return pl.pallas_call(
        _route_kernel,
        grid=(B, S // tl),
        in_specs=[pl.BlockSpec((1, 2 * P_HEADS, N_KEYS, tl), lambda b, i: (b, 0, 0, i))],
        out_specs=[spec] * 4,
        out_shape=[shp(F32), shp(F32), shp(BF16), shp(BF16)],
        compiler_params=_params(("parallel", "parallel")),
        name="route",
    )(scores)


def _peer_kernel(h_ref, e1_ref, nb_ref, e2_ref, r2_ref, u_ref, vt_ref, x1_ref, mod_ref, y_ref,
                 acc_ref, bc_ref, mt_ref):
    e = pl.program_id(2)
    rows_per_step = TE_PEER // N_KEYS
    tm = h_ref.shape[1]

    @pl.when(e == 0)
    def _():
        acc_ref[...] = jnp.zeros_like(acc_ref)

    a = _dot_nt(u_ref[...], h_ref[0])

    for hd in range(P_HEADS):
        for r in range(rows_per_step):
            idx = 2 * (hd * rows_per_step + r)
            bc_ref[idx] = jnp.broadcast_to(nb_ref[0, hd, r:r + 1, :], (BF16_ROWS, tm)).astype(BF16)
            bc_ref[idx + 1] = jnp.broadcast_to(e1_ref[0, hd, r:r + 1, :], (BF16_ROWS, tm)).astype(BF16)

    zero = jnp.zeros((BF16_ROWS, tm), BF16)
    for ib in range(N_KEYS // BF16_ROWS):
        lo = ib * BF16_ROWS
        w = [None] * rows_per_step
        for hd in range(P_HEADS):
            r2 = r2_ref[0, hd, lo:lo + BF16_ROWS, :]
            e2 = e2_ref[0, hd, lo:lo + BF16_ROWS, :]
            for r in range(rows_per_step):
                idx = 2 * (hd * rows_per_step + r)
                t = jnp.where(r2 < bc_ref[idx], e2, zero) * bc_ref[idx + 1]
                w[r] = t if w[r] is None else w[r] + t
        for r in range(rows_per_step):
            ar = a[r * N_KEYS + lo:r * N_KEYS + lo + BF16_ROWS]
            g = 0.5 * ar * (1.0 + lax.erf(ar * (1.0 / math.sqrt(2.0))))
            mt_ref[r * N_KEYS + lo:r * N_KEYS + lo + BF16_ROWS, :] = w[r] * g.astype(BF16)
    acc_ref[...] += _dot(vt_ref[...], mt_ref[...])

    @pl.when(e == pl.num_programs(2) - 1)
    def _():
        g2 = mod_ref[0, :, 5 * D:6 * D]
        y_ref[0] = x1_ref[0] + g2 * acc_ref[...].T


def _peer(h2, e1, nb, e2, r2, x1, mod, lw):
    B, S, _ = h2.shape
    tm, te = TM_PEER, TE_PEER
    rows = te // N_KEYS
    tok = pl.BlockSpec((1, tm, D), lambda b, i, e: (b, i, 0))
    slab = pl.BlockSpec((1, P_HEADS, rows, tm), lambda b, i, e: (b, 0, e, i))
    dense = pl.BlockSpec((1, P_HEADS, N_KEYS, tm), lambda b, i, e: (b, 0, 0, i))
    return pl.pallas_call(
        _peer_kernel,
        grid=(B, S // tm, N_EXPERTS // te),
        in_specs=[tok, slab, slab, dense, dense,
                  pl.BlockSpec((te, D), lambda b, i, e: (e, 0)),
                  pl.BlockSpec((D, te), lambda b, i, e: (0, e)),
                  tok, pl.BlockSpec((1, 1, 6 * D), lambda b, i, e: (b, 0, 0))],
        out_specs=tok,
        out_shape=jax.ShapeDtypeStruct((B, S, D), F32),
        scratch_shapes=[pltpu.VMEM((D, tm), F32),
                        pltpu.VMEM((2 * P_HEADS * rows, BF16_ROWS, tm), BF16),
                        pltpu.VMEM((te, tm), BF16)],
        compiler_params=_params(("parallel", "parallel", "arbitrary")),
        name="peer",
    )(h2, e1, nb, e2, r2, lw["peer_u"], lw["peer_vt"], x1, mod)


def _block_ones(n, blk):
    i = jnp.arange(n) // blk
    return (i[:, None] == i[None, :]).astype(BF16)


def _pad_cols(w, n):
    return jnp.concatenate([w, jnp.zeros((w.shape[0], n - w.shape[1]), w.dtype)], axis=1)


def _layer_weights(l, norm1_g, norm2_g, w_in, a_q_norm, a_k_norm, cq_norm, ckv_norm, w_uq, w_ukv,
                   b_q_norm, b_k_norm, w_pa, w_pb, w_out, peer_wq, peer_k1, peer_k2, peer_u, peer_v):
    w = w_in[l]
    k0, k1 = w[:, 512:576], w[:, 576:640]
    v0, v1 = w[:, 640:704], w[:, 704:768]
    z64 = jnp.zeros((D, B_NOPE), F32)
    z32 = jnp.zeros((D, LANES - B_QK), F32)
    w_in_r = jnp.concatenate([w[:, 0:512], k0, k0, k1, k1, v0, v0, v1, v1, w[:, 768:1024], w[:, 1024:1280],
                              w[:, 1312:2336], w[:, 2336:3360], z64, w[:, 1280:1312], z32], axis=1).astype(BF16)
    uq = w_uq[l].reshape(Q_LORA, B_HEADS, B_QK)
    w_uq_r = jnp.pad(uq, ((0, 0), (0, 0), (0, LANES - B_QK))).reshape(Q_LORA, B_HEADS * LANES).astype(BF16)
    ukv = w_ukv[l].reshape(KV_LORA, B_HEADS, B_NOPE + B_V)
    w_uk_r = jnp.pad(ukv[:, :, :B_NOPE], ((0, 0), (0, 0), (0, LANES - B_NOPE))).reshape(KV_LORA, B_HEADS * LANES).astype(BF16)
    w_uv_r = ukv[:, :, B_NOPE:].reshape(KV_LORA, B_HEADS * B_V).astype(BF16)
    row = lambda a: a.reshape(1, -1)
    return {
        "norm1_g": row(norm1_g[l]), "norm2_g": row(norm2_g[l]), "w_in": w_in_r,
        "a_qg": row(jnp.tile(a_q_norm[l], A_HEADS)), "a_kg": row(jnp.tile(a_k_norm[l], 2 * A_KV)),
        "ones_q": _block_ones(A_HEADS * A_DH, A_DH), "ones_k": _block_ones(2 * A_KV * A_DH, A_DH),
        "cq_g": row(cq_norm[l]), "ckv_g": row(ckv_norm[l]),
        "w_uq": w_uq_r, "w_uk": w_uk_r, "w_uv": w_uv_r,
        "b_qg": row(_pad_cols(b_q_norm[l][None], LANES)), "b_kg": row(_pad_cols(b_k_norm[l][None], LANES)),
        "w_pa": w_pa[l].astype(BF16), "w_pb": w_pb[l].astype(BF16), "w_out": w_out[l].astype(BF16),
        "peer_wq": peer_wq[l].astype(BF16), "peer_k1": peer_k1[l].astype(BF16), "peer_k2": peer_k2[l].astype(BF16),
        "peer_u": peer_u[l].astype(BF16), "peer_vt": peer_v[l].astype(BF16).T,
    }


def _rope_tables(S):
    half = B_ROPE // 2
    inv = jnp.power(jnp.float32(ROPE_BASE), -jnp.arange(half, dtype=F32) / half)
    ang = jnp.arange(S, dtype=F32)[:, None] * inv[None, :]
    cos, sin = jnp.cos(ang), jnp.sin(ang)
    zeros = jnp.zeros((S, half), F32)
    pad = jnp.zeros((S, LANES - B_QK), F32)
    cos_t = jnp.concatenate([jnp.ones((S, B_NOPE), F32), cos, cos, pad + 1.0], axis=1)
    sin_lo = jnp.concatenate([jnp.zeros((S, B_NOPE), F32), -sin, zeros, pad], axis=1)
    sin_hi = jnp.concatenate([jnp.zeros((S, B_NOPE), F32), zeros, sin, pad], axis=1)
    return cos_t, sin_lo, sin_hi


def _trunk(x, mods, layers, sinks):
    rope = _rope_tables(x.shape[1])
    for l in range(DEPTH):
        lw = layers[l]
        mod = mods[l]
        qa, ka, va, qb, kb, vb, ga, gb = _inproj(x, mod, lw, rope)
        ya = _attn_a(qa, ka, va, sinks[l])
        yb = _attn_b(qb, kb, vb)
        x1, h2, scores = _mixout(x, ya, yb, ga, gb, mod, lw)
        e1, nb, e2, r2 = _route(scores)
        x = _peer(h2, e1, nb, e2, r2, x1, mod, lw)
    return x


def kernel(x_prompt, x_sample, c_prompt, c_sample, ada_w, ada_b, norm1_g, norm2_g, w_in, a_q_norm, a_k_norm, a_sink, cq_norm, ckv_norm, w_uq, w_ukv, b_q_norm, b_k_norm, w_pa, w_pb, w_out, peer_wq, peer_k1, peer_k2, peer_u, peer_v):
    nb_prompt = c_prompt.shape[0]
    mod_all = _ada(jnp.concatenate([c_prompt, c_sample], axis=0), ada_w, ada_b)
    mods_p = [mod_all[l, :nb_prompt].reshape(nb_prompt, 1, 6 * D) for l in range(DEPTH)]
    mods_s = [mod_all[l, nb_prompt:].reshape(-1, 1, 6 * D) for l in range(DEPTH)]
    layers = [_layer_weights(l, norm1_g, norm2_g, w_in, a_q_norm, a_k_norm, cq_norm, ckv_norm, w_uq, w_ukv,
                             b_q_norm, b_k_norm, w_pa, w_pb, w_out, peer_wq, peer_k1, peer_k2, peer_u, peer_v)
              for l in range(DEPTH)]
    sinks = [a_sink[l] for l in range(DEPTH)]
    y_prompt = _trunk(x_prompt, mods_p, layers, sinks)
    y_sample = _trunk(x_sample, mods_s, layers, sinks)
    return (y_prompt, y_sample)
```

```python
import functools
import math

import jax
import jax.numpy as jnp
from jax import lax
from jax.experimental import pallas as pl
from jax.experimental.pallas import tpu as pltpu

F32 = jnp.float32
BF16 = jnp.bfloat16

D = 1024
DEPTH = 2
EPS = 1e-6
NEG_INF = -1e30
LOG2E = 1.4426950408889634
LANES = 128
BF16_ROWS = 16
BLOCK = 128
A_HEADS, A_KV, A_DH = 8, 2, 64
WINDOW = 128
B_HEADS, B_NOPE, B_ROPE, B_V = 8, 64, 32, 64
B_QK = B_NOPE + B_ROPE
Q_LORA = KV_LORA = 256
ROPE_BASE = 10000.0
P_HEADS, P_HALF, N_KEYS, P_TOPK = 8, 128, 128, 16
N_EXPERTS = N_KEYS * N_KEYS

C_QA, C_KA, C_VA, C_CQ, C_CKV, C_GA, C_GB, C_KR, N_IN = 0, 512, 768, 1024, 1280, 1536, 2560, 3584, 3712

VMEM_LIMIT = 56 * 1024 * 1024

TM_IN = 512
TM_MIX = 256
TQ_B = 512
TK_B = 512
TL_ROUTE = 128
TM_PEER = 512
TE_PEER = 2048
SUB_PEER = 512


def _dot(a, b):
    return jnp.dot(a, b, preferred_element_type=F32)


def _dot_nt(a, b):
    return lax.dot_general(a, b, (((1,), (1,)), ((), ())), preferred_element_type=F32)


def _params(sem):
    return pltpu.CompilerParams(dimension_semantics=sem, vmem_limit_bytes=VMEM_LIMIT)


def _ada_kernel(c_ref, w_ref, b_ref, o_ref):
    c = c_ref[...]
    s = c * jax.nn.sigmoid(c)
    w = w_ref[0]
    s_hi = s.astype(BF16)
    s_lo = (s - s_hi.astype(F32)).astype(BF16)
    w_hi = w.astype(BF16)
    w_lo = (w - w_hi.astype(F32)).astype(BF16)
    o_ref[0] = _dot(s_hi, w_hi) + _dot(s_lo, w_hi) + _dot(s_hi, w_lo) + b_ref[0]


def _ada(c_all, ada_w, ada_b):
    n = c_all.shape[0]
    tn = 1536
    return pl.pallas_call(
        _ada_kernel,
        grid=(DEPTH, 6 * D // tn),
        in_specs=[
            pl.BlockSpec((n, D), lambda l, j: (0, 0)),
            pl.BlockSpec((1, D, tn), lambda l, j: (l, 0, j)),
            pl.BlockSpec((1, 1, tn), lambda l, j: (l, 0, j)),
        ],
        out_specs=pl.BlockSpec((1, n, tn), lambda l, j: (l, 0, j)),
        out_shape=jax.ShapeDtypeStruct((DEPTH, n, 6 * D), F32),
        compiler_params=_params(("arbitrary", "arbitrary")),
        name="ada",
    )(c_all, ada_w, ada_b.reshape(DEPTH, 1, 6 * D))


def _rope_tile(x, cos, sin_lo, sin_hi):
    return x * cos + pltpu.roll(x, LANES - B_ROPE // 2, 1) * sin_lo + pltpu.roll(x, B_ROPE // 2, 1) * sin_hi


def _inproj_kernel(x_ref, mod_ref, g1_ref, win_ref, aqg_ref, akg_ref, onesq_ref, onesk_ref,
                   cqg_ref, ckvg_ref, wuq_ref, wuk_ref, wuv_ref, bqg_ref, bkg_ref,
                   cos_ref, sinlo_ref, sinhi_ref,
                   qa_ref, ka_ref, va_ref, qb_ref, kb_ref, vb_ref, ga_ref, gb_ref):
    x = x_ref[0]
    sh1 = mod_ref[0, :, 0:D]
    sc1 = mod_ref[0, :, D:2 * D]
    ms = jnp.mean(x * x, axis=-1, keepdims=True)
    h = (x * lax.rsqrt(ms + EPS)) * g1_ref[...]
    h = h * (1.0 + sc1) + sh1
    hb = h.astype(BF16)

    def proj(a, b):
        return _dot(hb, win_ref[:, a:b])

    z = proj(C_QA, C_KA)
    ss = _dot((z * z).astype(BF16), onesq_ref[...])
    qa_ref[0] = (z * lax.rsqrt(ss * (1.0 / A_DH) + EPS) * aqg_ref[...] * (A_DH ** -0.5)).astype(BF16)
    z = proj(C_KA, C_VA)
    ss = _dot((z * z).astype(BF16), onesk_ref[...])
    ka_ref[0] = (z * lax.rsqrt(ss * (1.0 / A_DH) + EPS) * akg_ref[...]).astype(BF16)
    va_ref[0] = proj(C_VA, C_CQ).astype(BF16)

    cos = cos_ref[...]
    sin_lo = sinlo_ref[...]
    sin_hi = sinhi_ref[...]

    z = proj(C_CQ, C_CKV)
    ms = jnp.mean(z * z, axis=-1, keepdims=True)
    cqn = (z * lax.rsqrt(ms + EPS) * cqg_ref[...]).astype(BF16)
    q = _dot(cqn, wuq_ref[...])
    bqg = bqg_ref[...]
    for hd in range(B_HEADS):
        qh = q[:, hd * LANES:(hd + 1) * LANES]
        ssh = jnp.sum(qh * qh, axis=-1, keepdims=True) * (1.0 / B_QK)
        qh = qh * lax.rsqrt(ssh + EPS) * bqg
        qh = _rope_tile(qh, cos, sin_lo, sin_hi) * (B_QK ** -0.5 * LOG2E)
        qb_ref[0, :, hd * LANES:(hd + 1) * LANES] = qh.astype(BF16)

    z = proj(C_CKV, C_GA)
    ms = jnp.mean(z * z, axis=-1, keepdims=True)
    ckvn = (z * lax.rsqrt(ms + EPS) * ckvg_ref[...]).astype(BF16)
    kn = _dot(ckvn, wuk_ref[...])
    vb_ref[0] = _dot(ckvn, wuv_ref[...]).astype(BF16)
    kr = proj(C_KR, N_IN)
    bkg = bkg_ref[...]
    for hd in range(B_HEADS):
        kh = kn[:, hd * LANES:(hd + 1) * LANES] + kr
        ssh = jnp.sum(kh * kh, axis=-1, keepdims=True) * (1.0 / B_QK)
        kh = kh * lax.rsqrt(ssh + EPS) * bkg
        kh = _rope_tile(kh, cos, sin_lo, sin_hi)
        kb_ref[0, :, hd * LANES:(hd + 1) * LANES] = kh.astype(BF16)

    ga_ref[0] = jax.nn.sigmoid(proj(C_GA, C_GB)).astype(BF16)
    gb_ref[0] = jax.nn.sigmoid(proj(C_GB, C_KR)).astype(BF16)


def _inproj(x, mod, lw, rope):
    B, S, _ = x.shape
    tm = TM_IN
    tok = lambda w: pl.BlockSpec((1, tm, w), lambda b, i: (b, i, 0))
    full = lambda a: pl.BlockSpec(a.shape, lambda b, i: (0,) * a.ndim)
    rope_spec = pl.BlockSpec((tm, LANES), lambda b, i: (i, 0))
    consts = [lw["norm1_g"], lw["w_in"], lw["a_qg"], lw["a_kg"], lw["ones_q"], lw["ones_k"],
              lw["cq_g"], lw["ckv_g"], lw["w_uq"], lw["w_uk"], lw["w_uv"], lw["b_qg"], lw["b_kg"]]
    widths = [512, 256, 256, 1024, 1024, 512, 1024, 1024]
    return pl.pallas_call(
        _inproj_kernel,
        grid=(B, S // tm),
        in_specs=[tok(D), pl.BlockSpec((1, 1, 6 * D), lambda b, i: (b, 0, 0))]
                 + [full(a) for a in consts] + [rope_spec] * 3,
        out_specs=[tok(w) for w in widths],
        out_shape=[jax.ShapeDtypeStruct((B, S, w), BF16) for w in widths],
        compiler_params=_params(("parallel", "parallel")),
        name="inproj",
    )(x, mod, *consts, *rope)


def _attn_a_kernel(sink_ref, q_ref, k_ref, v_ref, o_ref):
    n = pl.program_id(1)
    S = k_ref.shape[1]
    span = 3 * BLOCK
    group = A_HEADS // A_KV
    start = pl.multiple_of(jnp.clip((n - 1) * BLOCK, 0, S - span), BLOCK)
    kw = k_ref[0, pl.ds(start, span), :]
    vw = v_ref[0, pl.ds(start, span), :]
    rows = group * BLOCK
    lane = lax.broadcasted_iota(jnp.int32, (BLOCK, LANES), 1)
    qpos = n * BLOCK + lax.broadcasted_iota(jnp.int32, (rows, span), 0) % BLOCK
    kpos = start + lax.broadcasted_iota(jnp.int32, (rows, span), 1)
    dist = jnp.abs(qpos - kpos)
    valid = dist <= WINDOW
    distf = dist.astype(F32)
    rowhead = lax.broadcasted_iota(jnp.int32, (rows, 1), 0) // BLOCK
    for j in range(A_KV):
        kd = kw[:, j * LANES:(j + 1) * LANES]
        vd = vw[:, j * LANES:(j + 1) * LANES]
        parts = []
        for pp in range(group // 2):
            p = (group // 2) * j + pp
            qp = q_ref[0, :, p * LANES:(p + 1) * LANES]
            zero = jnp.zeros_like(qp)
            parts.append(jnp.where(lane < A_DH, qp, zero))
            parts.append(jnp.where(lane >= A_DH, qp, zero))
        q4 = jnp.concatenate(parts, axis=0)
        s = _dot_nt(q4, kd)
        head = (group * j + rowhead + 1).astype(F32)
        slope = jnp.exp2(-8.0 * head / A_HEADS)
        sink = jnp.zeros((rows, 1), F32)
        for g in range(group):
            sink = jnp.where(rowhead == g, sink_ref[group * j + g], sink)
        logits = jnp.where(valid, s - slope * distf, NEG_INF)
        m = jnp.maximum(jnp.max(logits, axis=-1, keepdims=True), sink)
        p = jnp.exp(logits - m)
        denom = jnp.sum(p, axis=-1, keepdims=True) + jnp.exp(sink - m)
        o = _dot((p / denom).astype(BF16), vd)
        for pp in range(group // 2):
            oe = o[(2 * pp) * BLOCK:(2 * pp + 1) * BLOCK]
            oo = o[(2 * pp + 1) * BLOCK:(2 * pp + 2) * BLOCK]
            p_idx = (group // 2) * j + pp
            o_ref[0, :, p_idx * LANES:(p_idx + 1) * LANES] = jnp.where(lane < A_DH, oe, oo).astype(BF16)


def _attn_a(qa, ka, va, sink):
    B, S, _ = qa.shape
    return pl.pallas_call(
        _attn_a_kernel,
        grid=(B, S // BLOCK),
        in_specs=[
            pl.BlockSpec(memory_space=pltpu.SMEM),
            pl.BlockSpec((1, BLOCK, 512), lambda b, n: (b, n, 0)),
            pl.BlockSpec((1, S, 256), lambda b, n: (b, 0, 0)),
            pl.BlockSpec((1, S, 256), lambda b, n: (b, 0, 0)),
        ],
        out_specs=pl.BlockSpec((1, BLOCK, 512), lambda b, n: (b, n, 0)),
        out_shape=jax.ShapeDtypeStruct((B, S, 512), BF16),
        compiler_params=_params(("parallel", "arbitrary")),
        name="attn_a",
    )(sink, qa, ka, va)


def _attn_b_kernel(q_ref, k_ref, v_ref, o_ref, s_ref, p_ref):
    S = k_ref.shape[1]
    tq = q_ref.shape[1]
    nk = S // TK_B
    lane = lax.broadcasted_iota(jnp.int32, (tq, LANES), 1)
    outs = []
    for hh in range(2):
        q = q_ref[0, :, hh * LANES:(hh + 1) * LANES]

        def kchunk(j, hh=hh):
            return k_ref[0, pl.ds(pl.multiple_of(j * TK_B, TK_B), TK_B), hh * LANES:(hh + 1) * LANES]

        def vchunk(j):
            return v_ref[0, pl.ds(pl.multiple_of(j * TK_B, TK_B), TK_B), :]

        def stage(j, slot, carry, q=q, kchunk=kchunk, vchunk=vchunk):
            m, l, alpha_prev, acc = carry
            s_ref[1 - slot] = _dot_nt(q, kchunk(jnp.minimum(j + 1, nk - 1)))
            acc = acc * alpha_prev + _dot(p_ref[1 - slot], vchunk(jnp.maximum(j - 1, 0)))
            m_new = jnp.maximum(m, jnp.max(s_ref[slot], axis=-1, keepdims=True))
            alpha = jnp.exp2(m - m_new)
            p = jnp.exp2(s_ref[slot] - m_new)
            p_ref[slot] = p.astype(BF16)
            psum = p[:, 0:LANES]
            for c in range(1, TK_B // LANES):
                psum = psum + p[:, c * LANES:(c + 1) * LANES]
            return m_new, alpha * l + psum, alpha, acc

        def body(jj, carry, stage=stage):
            carry = stage(2 * jj, 0, carry)
            return stage(2 * jj + 1, 1, carry)

        s_ref[0] = _dot_nt(q, kchunk(0))
        p_ref[1] = jnp.zeros(p_ref.shape[1:], BF16)
        init = (jnp.full((tq, 1), -jnp.inf, F32), jnp.zeros((tq, LANES), F32), jnp.ones((tq, 1), F32),
                jnp.zeros((tq, LANES), F32))
        carry = init
        for jj in range(nk // 2):
            carry = body(jj, carry)
        m, l, alpha, acc = carry
        acc = acc * alpha + _dot(p_ref[1], vchunk(nk - 1))
        outs.append(acc / jnp.sum(l, axis=-1, keepdims=True))
    o_ref[0] = jnp.where(lane < B_V, outs[0], outs[1]).astype(BF16)


def _attn_b(qb, kb, vb):
    B, S, _ = qb.shape
    tq = TQ_B
    return pl.pallas_call(
        _attn_b_kernel,
        grid=(B, B_HEADS // 2, S // tq),
        in_specs=[
            pl.BlockSpec((1, tq, 2 * LANES), lambda b, p, i: (b, i, p)),
            pl.BlockSpec((1, S, 2 * LANES), lambda b, p, i: (b, 0, p)),
            pl.BlockSpec((1, S, LANES), lambda b, p, i: (b, 0, p)),
        ],
        out_specs=pl.BlockSpec((1, tq, LANES), lambda b, p, i: (b, i, p)),
        out_shape=jax.ShapeDtypeStruct((B, S, B_HEADS * B_V), BF16),
        scratch_shapes=[pltpu.VMEM((2, tq, TK_B), F32), pltpu.VMEM((2, tq, TK_B), BF16)],
        compiler_params=_params(("parallel", "parallel", "arbitrary")),
        name="attn_b",
    )(qb, kb, vb)


def _mixout_kernel(x_ref, ya_ref, yb_ref, ga_ref, gb_ref, mod_ref, wpa_ref, wpb_ref, wout_ref,
                   g2_ref, wq_ref, k1_ref, k2_ref, x1_ref, h2_ref, sc_ref):
    ma = _dot(ya_ref[0], wpa_ref[...])
    mb = _dot(yb_ref[0], wpb_ref[...])
    m = ga_ref[0].astype(F32) * ma + gb_ref[0].astype(F32) * mb
    mo = _dot(m.astype(BF16), wout_ref[...])
    g1 = mod_ref[0, :, 2 * D:3 * D]
    sh2 = mod_ref[0, :, 3 * D:4 * D]
    sc2 = mod_ref[0, :, 4 * D:5 * D]
    x1 = x_ref[0] + g1 * mo
    x1_ref[0] = x1
    ms = jnp.mean(x1 * x1, axis=-1, keepdims=True)
    h2 = (x1 * lax.rsqrt(ms + EPS)) * g2_ref[...]
    h2 = (h2 * (1.0 + sc2) + sh2).astype(BF16)
    h2_ref[0] = h2
    for hd in range(P_HEADS):
        for c, kref in enumerate((k1_ref, k2_ref)):
            col = (2 * hd + c) * P_HALF
            qhc = _dot(h2, wq_ref[:, col:col + P_HALF]).astype(BF16)
            sc_ref[0, 2 * hd + c] = _dot_nt(kref[...], qhc)


def _mixout(x, ya, yb, ga, gb, mod, lw):
    B, S, _ = x.shape
    tm = TM_MIX
    tok = lambda w: pl.BlockSpec((1, tm, w), lambda b, i: (b, i, 0))
    full = lambda a: pl.BlockSpec(a.shape, lambda b, i: (0,) * a.ndim)
    consts = [lw["w_pa"], lw["w_pb"], lw["w_out"], lw["norm2_g"], lw["peer_wq"], lw["peer_k1"], lw["peer_k2"]]
    return pl.pallas_call(
        _mixout_kernel,
        grid=(B, S // tm),
        in_specs=[tok(D), tok(512), tok(512), tok(D), tok(D),
                  pl.BlockSpec((1, 1, 6 * D), lambda b, i: (b, 0, 0))] + [full(a) for a in consts],
        out_specs=[tok(D), tok(D), pl.BlockSpec((1, 2 * P_HEADS, N_KEYS, tm), lambda b, i: (b, 0, 0, i))],
        out_shape=[jax.ShapeDtypeStruct((B, S, D), F32), jax.ShapeDtypeStruct((B, S, D), BF16),
                   jax.ShapeDtypeStruct((B, 2 * P_HEADS, N_KEYS, S), F32)],
        compiler_params=_params(("parallel", "parallel")),
        name="mixout",
    )(x, ya, yb, ga, gb, mod, *consts)


def _extract_topk(s, key_iota, break_ties):
    rank = jnp.full(s.shape, float(P_TOPK), F32)
    vals = []
    for a in range(P_TOPK):
        m = jnp.max(s, axis=0, keepdims=True)
        hit = s == m
        if break_ties:
            idx = jnp.min(jnp.where(hit, key_iota, float(N_KEYS)), axis=0, keepdims=True)
            hit = key_iota == idx
        rank = jnp.where(hit, float(a), rank)
        s = jnp.where(hit, -jnp.inf, s)
        vals.append(m)
    return vals, rank


def _count(mask):
    return jnp.sum(jnp.where(mask, 1.0, 0.0), axis=0, keepdims=True)


def _route_head(s1, s2, key_iota, b8, break_ties):
    tl = s1.shape[1]
    v1, rank1 = _extract_topk(s1, key_iota, break_ties)
    v2, rank2 = _extract_topk(s2, key_iota, break_ties)
    v2lo = jnp.concatenate(v2[:8], axis=0)
    v2hi = jnp.concatenate(v2[8:], axis=0)
    v1hi = jnp.concatenate(v1[8:], axis=0)
    tiles = [v1[0] + v2lo, v1[0] + v2hi, v1[1] + v2lo]
    flats = [b8, b8 + 8.0, b8 + float(P_TOPK)]
    valid = [None, None, None]
    for a in range(2, 8):
        ok = b8 < float(P_TOPK // (a + 1))
        tiles.append(jnp.where(ok, v1[a] + v2lo, -jnp.inf))
        flats.append(b8 + float(a * P_TOPK))
        valid.append(ok)
    tiles.append(v1hi + v2[0])
    flats.append((b8 + 8.0) * float(P_TOPK))
    valid.append(None)
    big = float(P_TOPK * P_TOPK)
    m0 = None
    z = jnp.zeros((1, tl), F32)
    for k in range(P_TOPK):
        m = tiles[0]
        for t in tiles[1:]:
            m = jnp.maximum(m, t)
        m = jnp.max(m, axis=0, keepdims=True)
        if break_ties:
            idx = None
            for t, f in zip(tiles, flats):
                cur = jnp.where(t == m, f, big)
                idx = cur if idx is None else jnp.minimum(idx, cur)
            idx = jnp.min(idx, axis=0, keepdims=True)
            tiles = [jnp.where(f == idx, -jnp.inf, t) for t, f in zip(tiles, flats)]
        else:
            tiles = [jnp.where(t == m, -jnp.inf, t) for t in tiles]
        if m0 is None:
            m0 = m
        z = z + jnp.exp(m - m0)
    taken = [t == -jnp.inf if ok is None else (t == -jnp.inf) & ok for t, ok in zip(tiles, valid)]
    nb = [_count(taken[0]) + _count(taken[1]), _count(taken[2])] + [_count(taken[a + 1]) for a in range(2, 8)]
    last = jnp.where(taken[9], 1.0, 0.0)
    nb += [last[a:a + 1] for a in range(8)]
    total = nb[0]
    for x in nb[1:]:
        total = total + x
    nb1 = jnp.zeros((N_KEYS, tl), F32)
    for a in range(P_TOPK):
        nb1 = jnp.where(rank1 == float(a), nb[a], nb1)
    want = float(P_TOPK)
    bad = (_count(rank1 < want) != want) | (_count(rank2 < want) != want) | (total != want)
    e1 = jnp.exp(s1 - v1[0]) / z
    e2 = jnp.exp(s2 - v2[0])
    return e1, nb1, e2, rank2, bad


def _route_kernel(sc_ref, e1_ref, nb_ref, e2_ref, r2_ref):
    tl = sc_ref.shape[3]
    key_iota = lax.broadcasted_iota(jnp.int32, (N_KEYS, tl), 0).astype(F32)
    b8 = lax.broadcasted_iota(jnp.int32, (8, tl), 0).astype(F32)

    def head(hd, carry):
        def run(break_ties):
            e1, nb1, e2, rank2, bad = _route_head(sc_ref[0, 2 * hd], sc_ref[0, 2 * hd + 1], key_iota, b8, break_ties)
            e1_ref[0, hd] = e1
            nb_ref[0, hd] = nb1
            e2_ref[0, hd] = e2.astype(BF16)
            r2_ref[0, hd] = rank2.astype(BF16)
            return bad

        bad = run(False)

        @pl.when(jnp.max(jnp.where(bad, 1.0, 0.0)) > 0.0)
        def _():
            run(True)

        return carry

    lax.fori_loop(0, P_HEADS, head, 0)


def _route(scores):
    B, _, _, S = scores.shape
    tl = TL_ROUTE
    spec = pl.BlockSpec((1, P_HEADS, N_KEYS, tl), lambda b, i: (b, 0, 0, i))
    shp = lambda dt: jax.ShapeDtypeStruct((B, P_HEADS, N_KEYS, S), dt)
    return pl.pallas_call(
        _route_kernel,
        grid=(B, S // tl),
        in_specs=[pl.BlockSpec((1, 2 * P_HEADS, N_KEYS, tl), lambda b, i: (b, 0, 0, i))],
        out_specs=[spec] * 4,
        out_shape=[shp(F32), shp(F32), shp(BF16), shp(BF16)],
        compiler_params=_params(("parallel", "parallel")),
        name="route",
    )(scores)


def _peer_kernel(h_ref, e1_ref, nb_ref, e2_ref, r2_ref, u_ref, vt_ref, x1_ref, mod_ref, y_ref,
                 acc_ref, bc_ref, mt_ref, a_ref):
    e = pl.program_id(2)
    rows_sub = SUB_PEER // N_KEYS
    tm = h_ref.shape[1]

    @pl.when(e == 0)
    def _():
        acc_ref[...] = jnp.zeros_like(acc_ref)

    h = h_ref[0]
    zero = jnp.zeros((BF16_ROWS, tm), BF16)
    nsub = TE_PEER // SUB_PEER

    def scores(k):
        return _dot_nt(u_ref[k * SUB_PEER:(k + 1) * SUB_PEER, :], h)

    a_ref[0] = scores(0)
    for k in range(nsub):
        if k >= 1:
            acc_ref[...] += _dot(vt_ref[:, (k - 1) * SUB_PEER:k * SUB_PEER], mt_ref[k - 1])
        if k + 1 < nsub:
            a_ref[(k + 1) % 2] = scores(k + 1)

        for hd in range(P_HEADS):
            for r in range(rows_sub):
                idx = 2 * (hd * rows_sub + r)
                row = k * rows_sub + r
                bc_ref[k, idx] = jnp.broadcast_to(nb_ref[0, hd, row:row + 1, :], (BF16_ROWS, tm)).astype(BF16)
                bc_ref[k, idx + 1] = jnp.broadcast_to(e1_ref[0, hd, row:row + 1, :], (BF16_ROWS, tm)).astype(BF16)

        for ib in range(N_KEYS // BF16_ROWS):
            lo = ib * BF16_ROWS
            w = [None] * rows_sub
            for hd in range(P_HEADS):
                r2 = r2_ref[0, hd, lo:lo + BF16_ROWS, :]
                e2 = e2_ref[0, hd, lo:lo + BF16_ROWS, :]
                for r in range(rows_sub):
                    idx = 2 * (hd * rows_sub + r)
                    t = jnp.where(r2 < bc_ref[k, idx], e2, zero) * bc_ref[k, idx + 1]
                    w[r] = t if w[r] is None else w[r] + t
            for r in range(rows_sub):
                ar = a_ref[k % 2, r * N_KEYS + lo:r * N_KEYS + lo + BF16_ROWS, :]
                g = 0.5 * ar * (1.0 + lax.erf(ar * (1.0 / math.sqrt(2.0))))
                mt_ref[k, r * N_KEYS + lo:r * N_KEYS + lo + BF16_ROWS, :] = w[r] * g.astype(BF16)
    acc_ref[...] += _dot(vt_ref[:, (nsub - 1) * SUB_PEER:nsub * SUB_PEER], mt_ref[nsub - 1])

    @pl.when(e == pl.num_programs(2) - 1)
    def _():
        g2 = mod_ref[0, :, 5 * D:6 * D]
        y_ref[0] = x1_ref[0] + g2 * acc_ref[...].T


def _peer(h2, e1, nb, e2, r2, x1, mod, lw):
    B, S, _ = h2.shape
    tm, te = TM_PEER, TE_PEER
    rows = te // N_KEYS
    tok = pl.BlockSpec((1, tm, D), lambda b, i, e: (b, i, 0))
    slab = pl.BlockSpec((1, P_HEADS, rows, tm), lambda b, i, e: (b, 0, e, i))
    dense = pl.BlockSpec((1, P_HEADS, N_KEYS, tm), lambda b, i, e: (b, 0, 0, i))
    return pl.pallas_call(
        _peer_kernel,
        grid=(B, S // tm, N_EXPERTS // te),
        in_specs=[tok, slab, slab, dense, dense,
                  pl.BlockSpec((te, D), lambda b, i, e: (e, 0)),
                  pl.BlockSpec((D, te), lambda b, i, e: (0, e)),
                  tok, pl.BlockSpec((1, 1, 6 * D), lambda b, i, e: (b, 0, 0))],
        out_specs=tok,
        out_shape=jax.ShapeDtypeStruct((B, S, D), F32),
        scratch_shapes=[pltpu.VMEM((D, tm), F32),
                        pltpu.VMEM((te // SUB_PEER, 2 * P_HEADS * (SUB_PEER // N_KEYS), BF16_ROWS, tm), BF16),
                        pltpu.VMEM((te // SUB_PEER, SUB_PEER, tm), BF16),
                        pltpu.VMEM((2, SUB_PEER, tm), F32)],
        compiler_params=_params(("parallel", "parallel", "arbitrary")),
        name="peer",
    )(h2, e1, nb, e2, r2, lw["peer_u"], lw["peer_vt"], x1, mod)


def _block_ones(n, blk):
    i = jnp.arange(n) // blk
    return (i[:, None] == i[None, :]).astype(BF16)


def _pad_cols(w, n):
    return jnp.concatenate([w, jnp.zeros((w.shape[0], n - w.shape[1]), w.dtype)], axis=1)


def _layer_weights(l, norm1_g, norm2_g, w_in, a_q_norm, a_k_norm, cq_norm, ckv_norm, w_uq, w_ukv,
                   b_q_norm, b_k_norm, w_pa, w_pb, w_out, peer_wq, peer_k1, peer_k2, peer_u, peer_v):
    w = w_in[l]
    k0, k1 = w[:, 512:576], w[:, 576:640]
    v0, v1 = w[:, 640:704], w[:, 704:768]
    z64 = jnp.zeros((D, B_NOPE), F32)
    z32 = jnp.zeros((D, LANES - B_QK), F32)
    w_in_r = jnp.concatenate([w[:, 0:512], k0, k0, k1, k1, v0, v0, v1, v1, w[:, 768:1024], w[:, 1024:1280],
                              w[:, 1312:2336], w[:, 2336:3360], z64, w[:, 1280:1312], z32], axis=1).astype(BF16)
    uq = w_uq[l].reshape(Q_LORA, B_HEADS, B_QK)
    w_uq_r = jnp.pad(uq, ((0, 0), (0, 0), (0, LANES - B_QK))).reshape(Q_LORA, B_HEADS * LANES).astype(BF16)
    ukv = w_ukv[l].reshape(KV_LORA, B_HEADS, B_NOPE + B_V)
    w_uk_r = jnp.pad(ukv[:, :, :B_NOPE], ((0, 0), (0, 0), (0, LANES - B_NOPE))).reshape(KV_LORA, B_HEADS * LANES).astype(BF16)
    w_uv_r = ukv[:, :, B_NOPE:].reshape(KV_LORA, B_HEADS * B_V).astype(BF16)
    row = lambda a: a.reshape(1, -1)
    return {
        "norm1_g": row(norm1_g[l]), "norm2_g": row(norm2_g[l]), "w_in": w_in_r,
        "a_qg": row(jnp.tile(a_q_norm[l], A_HEADS)), "a_kg": row(jnp.tile(a_k_norm[l], 2 * A_KV)),
        "ones_q": _block_ones(A_HEADS * A_DH, A_DH), "ones_k": _block_ones(2 * A_KV * A_DH, A_DH),
        "cq_g": row(cq_norm[l]), "ckv_g": row(ckv_norm[l]),
        "w_uq": w_uq_r, "w_uk": w_uk_r, "w_uv": w_uv_r,
        "b_qg": row(_pad_cols(b_q_norm[l][None], LANES)), "b_kg": row(_pad_cols(b_k_norm[l][None], LANES)),
        "w_pa": w_pa[l].astype(BF16), "w_pb": w_pb[l].astype(BF16), "w_out": w_out[l].astype(BF16),
        "peer_wq": peer_wq[l].astype(BF16), "peer_k1": peer_k1[l].astype(BF16), "peer_k2": peer_k2[l].astype(BF16),
        "peer_u": peer_u[l].astype(BF16), "peer_vt": peer_v[l].astype(BF16).T,
    }


def _rope_tables(S):
    half = B_ROPE // 2
    inv = jnp.power(jnp.float32(ROPE_BASE), -jnp.arange(half, dtype=F32) / half)
    ang = jnp.arange(S, dtype=F32)[:, None] * inv[None, :]
    cos, sin = jnp.cos(ang), jnp.sin(ang)
    zeros = jnp.zeros((S, half), F32)
    pad = jnp.zeros((S, LANES - B_QK), F32)
    cos_t = jnp.concatenate([jnp.ones((S, B_NOPE), F32), cos, cos, pad + 1.0], axis=1)
    sin_lo = jnp.concatenate([jnp.zeros((S, B_NOPE), F32), -sin, zeros, pad], axis=1)
    sin_hi = jnp.concatenate([jnp.zeros((S, B_NOPE), F32), zeros, sin, pad], axis=1)
    return cos_t, sin_lo, sin_hi


def _trunk(x, mods, layers, sinks):
    rope = _rope_tables(x.shape[1])
    for l in range(DEPTH):
        lw = layers[l]
        mod = mods[l]
        qa, ka, va, qb, kb, vb, ga, gb = _inproj(x, mod, lw, rope)
        ya = _attn_a(qa, ka, va, sinks[l])
        yb = _attn_b(qb, kb, vb)
        x1, h2, scores = _mixout(x, ya, yb, ga, gb, mod, lw)
        e1, nb, e2, r2 = _route(scores)
        x = _peer(h2, e1, nb, e2, r2, x1, mod, lw)
    return x


def kernel(x_prompt, x_sample, c_prompt, c_sample, ada_w, ada_b, norm1_g, norm2_g, w_in, a_q_norm, a_k_norm, a_sink, cq_norm, ckv_norm, w_uq, w_ukv, b_q_norm, b_k_norm, w_pa, w_pb, w_out, peer_wq, peer_k1, peer_k2, peer_u, peer_v):
    nb_prompt = c_prompt.shape[0]
    mod_all = _ada(jnp.concatenate([c_prompt, c_sample], axis=0), ada_w, ada_b)
    mods_p = [mod_all[l, :nb_prompt].reshape(nb_prompt, 1, 6 * D) for l in range(DEPTH)]
    mods_s = [mod_all[l, nb_prompt:].reshape(-1, 1, 6 * D) for l in range(DEPTH)]
    layers = [_layer_weights(l, norm1_g, norm2_g, w_in, a_q_norm, a_k_norm, cq_norm, ckv_norm, w_uq, w_ukv,
                             b_q_norm, b_k_norm, w_pa, w_pb, w_out, peer_wq, peer_k1, peer_k2, peer_u, peer_v)
              for l in range(DEPTH)]
    sinks = [a_sink[l] for l in range(DEPTH)]
    y_prompt = _trunk(x_prompt, mods_p, layers, sinks)
    y_sample = _trunk(x_sample, mods_s, layers, sinks)
    return (y_prompt, y_sample)
```

```python
import functools
import math

import jax
import jax.numpy as jnp
from jax import lax
from jax.experimental import pallas as pl
from jax.experimental.pallas import tpu as pltpu

F32 = jnp.float32
BF16 = jnp.bfloat16

D = 1024
DEPTH = 2
EPS = 1e-6
NEG_INF = -1e30
LOG2E = 1.4426950408889634
LANES = 128
BF16_ROWS = 16
BLOCK = 128
A_HEADS, A_KV, A_DH = 8, 2, 64
WINDOW = 128
B_HEADS, B_NOPE, B_ROPE, B_V = 8, 64, 32, 64
B_QK = B_NOPE + B_ROPE
Q_LORA = KV_LORA = 256
ROPE_BASE = 10000.0
P_HEADS, P_HALF, N_KEYS, P_TOPK = 8, 128, 128, 16
N_EXPERTS = N_KEYS * N_KEYS

C_QA, C_KA, C_VA, C_CQ, C_CKV, C_GA, C_GB, C_KR, N_IN = 0, 512, 768, 1024, 1280, 1536, 2560, 3584, 3712

VMEM_LIMIT = 56 * 1024 * 1024

TM_IN = 512
TM_MIX = 512
TQ_B = 512
TK_B = 512
TL_ROUTE = 128
TM_PEER = 512
TE_PEER = 2048
SUB_PEER = 512


def _dot(a, b):
    return jnp.dot(a, b, preferred_element_type=F32)


def _dot_nt(a, b):
    return lax.dot_general(a, b, (((1,), (1,)), ((), ())), preferred_element_type=F32)


def _params(sem):
    return pltpu.CompilerParams(dimension_semantics=sem, vmem_limit_bytes=VMEM_LIMIT)


def _ada_kernel(c_ref, w_ref, b_ref, o_ref):
    c = c_ref[...]
    s = c * jax.nn.sigmoid(c)
    w = w_ref[0]
    s_hi = s.astype(BF16)
    s_lo = (s - s_hi.astype(F32)).astype(BF16)
    w_hi = w.astype(BF16)
    w_lo = (w - w_hi.astype(F32)).astype(BF16)
    o_ref[0] = _dot(s_hi, w_hi) + _dot(s_lo, w_hi) + _dot(s_hi, w_lo) + b_ref[0]


def _ada(c_all, ada_w, ada_b):
    n = c_all.shape[0]
    tn = 1536
    return pl.pallas_call(
        _ada_kernel,
        grid=(DEPTH, 6 * D // tn),
        in_specs=[
            pl.BlockSpec((n, D), lambda l, j: (0, 0)),
            pl.BlockSpec((1, D, tn), lambda l, j: (l, 0, j)),
            pl.BlockSpec((1, 1, tn), lambda l, j: (l, 0, j)),
        ],
        out_specs=pl.BlockSpec((1, n, tn), lambda l, j: (l, 0, j)),
        out_shape=jax.ShapeDtypeStruct((DEPTH, n, 6 * D), F32),
        compiler_params=_params(("arbitrary", "arbitrary")),
        name="ada",
    )(c_all, ada_w, ada_b.reshape(DEPTH, 1, 6 * D))


def _rope_tile(x, cos, sin_lo, sin_hi):
    return x * cos + pltpu.roll(x, LANES - B_ROPE // 2, 1) * sin_lo + pltpu.roll(x, B_ROPE // 2, 1) * sin_hi


def _inproj_kernel(x_ref, mod_ref, g1_ref, win_ref, aqg_ref, akg_ref, onesq_ref, onesk_ref,
                   cqg_ref, ckvg_ref, wuq_ref, wuk_ref, wuv_ref, bqg_ref, bkg_ref,
                   cos_ref, sinlo_ref, sinhi_ref,
                   qa_ref, ka_ref, va_ref, qb_ref, kb_ref, vb_ref, ga_ref, gb_ref):
    x = x_ref[0]
    sh1 = mod_ref[0, :, 0:D]
    sc1 = mod_ref[0, :, D:2 * D]
    ms = jnp.mean(x * x, axis=-1, keepdims=True)
    h = (x * lax.rsqrt(ms + EPS)) * g1_ref[...]
    h = h * (1.0 + sc1) + sh1
    hb = h.astype(BF16)

    def proj(a, b):
        return _dot(hb, win_ref[:, a:b])

    z = proj(C_QA, C_KA)
    ss = _dot((z * z).astype(BF16), onesq_ref[...])
    qa_ref[0] = (z * lax.rsqrt(ss * (1.0 / A_DH) + EPS) * aqg_ref[...] * (A_DH ** -0.5)).astype(BF16)
    z = proj(C_KA, C_VA)
    ss = _dot((z * z).astype(BF16), onesk_ref[...])
    ka_ref[0] = (z * lax.rsqrt(ss * (1.0 / A_DH) + EPS) * akg_ref[...]).astype(BF16)
    va_ref[0] = proj(C_VA, C_CQ).astype(BF16)

    cos = cos_ref[...]
    sin_lo = sinlo_ref[...]
    sin_hi = sinhi_ref[...]

    z = proj(C_CQ, C_CKV)
    ms = jnp.mean(z * z, axis=-1, keepdims=True)
    cqn = (z * lax.rsqrt(ms + EPS) * cqg_ref[...]).astype(BF16)
    q = _dot(cqn, wuq_ref[...])
    bqg = bqg_ref[...]
    for hd in range(B_HEADS):
        qh = q[:, hd * LANES:(hd + 1) * LANES]
        ssh = jnp.sum(qh * qh, axis=-1, keepdims=True) * (1.0 / B_QK)
        qh = qh * lax.rsqrt(ssh + EPS) * bqg
        qh = _rope_tile(qh, cos, sin_lo, sin_hi) * (B_QK ** -0.5 * LOG2E)
        qb_ref[0, :, hd * LANES:(hd + 1) * LANES] = qh.astype(BF16)

    z = proj(C_CKV, C_GA)
    ms = jnp.mean(z * z, axis=-1, keepdims=True)
    ckvn = (z * lax.rsqrt(ms + EPS) * ckvg_ref[...]).astype(BF16)
    kn = _dot(ckvn, wuk_ref[...])
    vb_ref[0] = _dot(ckvn, wuv_ref[...]).astype(BF16)
    kr = proj(C_KR, N_IN)
    bkg = bkg_ref[...]
    for hd in range(B_HEADS):
        kh = kn[:, hd * LANES:(hd + 1) * LANES] + kr
        ssh = jnp.sum(kh * kh, axis=-1, keepdims=True) * (1.0 / B_QK)
        kh = kh * lax.rsqrt(ssh + EPS) * bkg
        kh = _rope_tile(kh, cos, sin_lo, sin_hi)
        kb_ref[0, :, hd * LANES:(hd + 1) * LANES] = kh.astype(BF16)

    ga_ref[0] = jax.nn.sigmoid(proj(C_GA, C_GB)).astype(BF16)
    gb_ref[0] = jax.nn.sigmoid(proj(C_GB, C_KR)).astype(BF16)


def _inproj(x, mod, lw, rope):
    B, S, _ = x.shape
    tm = TM_IN
    tok = lambda w: pl.BlockSpec((1, tm, w), lambda b, i: (b, i, 0))
    full = lambda a: pl.BlockSpec(a.shape, lambda b, i: (0,) * a.ndim)
    rope_spec = pl.BlockSpec((tm, LANES), lambda b, i: (i, 0))
    consts = [lw["norm1_g"], lw["w_in"], lw["a_qg"], lw["a_kg"], lw["ones_q"], lw["ones_k"],
              lw["cq_g"], lw["ckv_g"], lw["w_uq"], lw["w_uk"], lw["w_uv"], lw["b_qg"], lw["b_kg"]]
    widths = [512, 256, 256, 1024, 1024, 512, 1024, 1024]
    return pl.pallas_call(
        _inproj_kernel,
        grid=(B, S // tm),
        in_specs=[tok(D), pl.BlockSpec((1, 1, 6 * D), lambda b, i: (b, 0, 0))]
                 + [full(a) for a in consts] + [rope_spec] * 3,
        out_specs=[tok(w) for w in widths],
        out_shape=[jax.ShapeDtypeStruct((B, S, w), BF16) for w in widths],
        compiler_params=_params(("parallel", "parallel")),
        name="inproj",
    )(x, mod, *consts, *rope)


def _attn_a_kernel(sink_ref, q_ref, k_ref, v_ref, o_ref):
    n = pl.program_id(1)
    S = k_ref.shape[1]
    span = 3 * BLOCK
    group = A_HEADS // A_KV
    start = pl.multiple_of(jnp.clip((n - 1) * BLOCK, 0, S - span), BLOCK)
    kw = k_ref[0, pl.ds(start, span), :]
    vw = v_ref[0, pl.ds(start, span), :]
    rows = group * BLOCK
    lane = lax.broadcasted_iota(jnp.int32, (BLOCK, LANES), 1)
    qpos = n * BLOCK + lax.broadcasted_iota(jnp.int32, (rows, span), 0) % BLOCK
    kpos = start + lax.broadcasted_iota(jnp.int32, (rows, span), 1)
    dist = jnp.abs(qpos - kpos)
    valid = dist <= WINDOW
    distf = dist.astype(F32)
    rowhead = lax.broadcasted_iota(jnp.int32, (rows, 1), 0) // BLOCK
    for j in range(A_KV):
        kd = kw[:, j * LANES:(j + 1) * LANES]
        vd = vw[:, j * LANES:(j + 1) * LANES]
        parts = []
        for pp in range(group // 2):
            p = (group // 2) * j + pp
            qp = q_ref[0, :, p * LANES:(p + 1) * LANES]
            zero = jnp.zeros_like(qp)
            parts.append(jnp.where(lane < A_DH, qp, zero))
            parts.append(jnp.where(lane >= A_DH, qp, zero))
        q4 = jnp.concatenate(parts, axis=0)
        s = _dot_nt(q4, kd)
        head = (group * j + rowhead + 1).astype(F32)
        slope = jnp.exp2(-8.0 * head / A_HEADS)
        sink = jnp.zeros((rows, 1), F32)
        for g in range(group):
            sink = jnp.where(rowhead == g, sink_ref[group * j + g], sink)
        logits = jnp.where(valid, s - slope * distf, NEG_INF)
        m = jnp.maximum(jnp.max(logits, axis=-1, keepdims=True), sink)
        p = jnp.exp(logits - m)
        denom = jnp.sum(p, axis=-1, keepdims=True) + jnp.exp(sink - m)
        o = _dot((p / denom).astype(BF16), vd)
        for pp in range(group // 2):
            oe = o[(2 * pp) * BLOCK:(2 * pp + 1) * BLOCK]
            oo = o[(2 * pp + 1) * BLOCK:(2 * pp + 2) * BLOCK]
            p_idx = (group // 2) * j + pp
            o_ref[0, :, p_idx * LANES:(p_idx + 1) * LANES] = jnp.where(lane < A_DH, oe, oo).astype(BF16)


def _attn_a(qa, ka, va, sink):
    B, S, _ = qa.shape
    return pl.pallas_call(
        _attn_a_kernel,
        grid=(B, S // BLOCK),
        in_specs=[
            pl.BlockSpec(memory_space=pltpu.SMEM),
            pl.BlockSpec((1, BLOCK, 512), lambda b, n: (b, n, 0)),
            pl.BlockSpec((1, S, 256), lambda b, n: (b, 0, 0)),
            pl.BlockSpec((1, S, 256), lambda b, n: (b, 0, 0)),
        ],
        out_specs=pl.BlockSpec((1, BLOCK, 512), lambda b, n: (b, n, 0)),
        out_shape=jax.ShapeDtypeStruct((B, S, 512), BF16),
        compiler_params=_params(("parallel", "arbitrary")),
        name="attn_a",
    )(sink, qa, ka, va)


def _attn_b_kernel(q_ref, k_ref, v_ref, o_ref, s_ref, p_ref):
    S = k_ref.shape[1]
    tq = q_ref.shape[1]
    nk = S // TK_B
    lane = lax.broadcasted_iota(jnp.int32, (tq, LANES), 1)
    outs = []
    for hh in range(2):
        q = q_ref[0, :, hh * LANES:(hh + 1) * LANES]

        def kchunk(j, hh=hh):
            return k_ref[0, pl.ds(pl.multiple_of(j * TK_B, TK_B), TK_B), hh * LANES:(hh + 1) * LANES]

        def vchunk(j):
            return v_ref[0, pl.ds(pl.multiple_of(j * TK_B, TK_B), TK_B), :]

        def stage(j, slot, carry, q=q, kchunk=kchunk, vchunk=vchunk):
            m, l, alpha_prev, acc = carry
            s_ref[1 - slot] = _dot_nt(q, kchunk(jnp.minimum(j + 1, nk - 1)))
            acc = acc * alpha_prev + _dot(p_ref[1 - slot], vchunk(jnp.maximum(j - 1, 0)))
            m_new = jnp.maximum(m, jnp.max(s_ref[slot], axis=-1, keepdims=True))
            alpha = jnp.exp2(m - m_new)
            p = jnp.exp2(s_ref[slot] - m_new)
            p_ref[slot] = p.astype(BF16)
            psum = p[:, 0:LANES]
            for c in range(1, TK_B // LANES):
                psum = psum + p[:, c * LANES:(c + 1) * LANES]
            return m_new, alpha * l + psum, alpha, acc

        def body(jj, carry, stage=stage):
            carry = stage(2 * jj, 0, carry)
            return stage(2 * jj + 1, 1, carry)

        s_ref[0] = _dot_nt(q, kchunk(0))
        p_ref[1] = jnp.zeros(p_ref.shape[1:], BF16)
        init = (jnp.full((tq, 1), -jnp.inf, F32), jnp.zeros((tq, LANES), F32), jnp.ones((tq, 1), F32),
                jnp.zeros((tq, LANES), F32))
        carry = init
        for jj in range(nk // 2):
            carry = body(jj, carry)
        m, l, alpha, acc = carry
        acc = acc * alpha + _dot(p_ref[1], vchunk(nk - 1))
        outs.append(acc / jnp.sum(l, axis=-1, keepdims=True))
    o_ref[0] = jnp.where(lane < B_V, outs[0], outs[1]).astype(BF16)


def _attn_b(qb, kb, vb):
    B, S, _ = qb.shape
    tq = TQ_B
    return pl.pallas_call(
        _attn_b_kernel,
        grid=(B, B_HEADS // 2, S // tq),
        in_specs=[
            pl.BlockSpec((1, tq, 2 * LANES), lambda b, p, i: (b, i, p)),
            pl.BlockSpec((1, S, 2 * LANES), lambda b, p, i: (b, 0, p)),
            pl.BlockSpec((1, S, LANES), lambda b, p, i: (b, 0, p)),
        ],
        out_specs=pl.BlockSpec((1, tq, LANES), lambda b, p, i: (b, i, p)),
        out_shape=jax.ShapeDtypeStruct((B, S, B_HEADS * B_V), BF16),
        scratch_shapes=[pltpu.VMEM((2, tq, TK_B), F32), pltpu.VMEM((2, tq, TK_B), BF16)],
        compiler_params=_params(("parallel", "parallel", "arbitrary")),
        name="attn_b",
    )(qb, kb, vb)


def _mixout_kernel(x_ref, ya_ref, yb_ref, ga_ref, gb_ref, mod_ref, wpa_ref, wpb_ref, wout_ref,
                   g2_ref, wq_ref, k1_ref, k2_ref, x1_ref, h2_ref, sc_ref):
    ma = _dot(ya_ref[0], wpa_ref[...])
    mb = _dot(yb_ref[0], wpb_ref[...])
    m = ga_ref[0].astype(F32) * ma + gb_ref[0].astype(F32) * mb
    mo = _dot(m.astype(BF16), wout_ref[...])
    g1 = mod_ref[0, :, 2 * D:3 * D]
    sh2 = mod_ref[0, :, 3 * D:4 * D]
    sc2 = mod_ref[0, :, 4 * D:5 * D]
    x1 = x_ref[0] + g1 * mo
    x1_ref[0] = x1
    ms = jnp.mean(x1 * x1, axis=-1, keepdims=True)
    h2 = (x1 * lax.rsqrt(ms + EPS)) * g2_ref[...]
    h2 = (h2 * (1.0 + sc2) + sh2).astype(BF16)
    h2_ref[0] = h2
    q = _dot(h2, wq_ref[...]).astype(BF16)
    for hd in range(P_HEADS):
        for c, kref in enumerate((k1_ref, k2_ref)):
            col = (2 * hd + c) * P_HALF
            sc_ref[0, 2 * hd + c] = _dot_nt(kref[...], q[:, col:col + P_HALF])


def _mixout(x, ya, yb, ga, gb, mod, lw):
    B, S, _ = x.shape
    tm = TM_MIX
    tok = lambda w: pl.BlockSpec((1, tm, w), lambda b, i: (b, i, 0))
    full = lambda a: pl.BlockSpec(a.shape, lambda b, i: (0,) * a.ndim)
    consts = [lw["w_pa"], lw["w_pb"], lw["w_out"], lw["norm2_g"], lw["peer_wq"], lw["peer_k1"], lw["peer_k2"]]
    return pl.pallas_call(
        _mixout_kernel,
        grid=(B, S // tm),
        in_specs=[tok(D), tok(512), tok(512), tok(D), tok(D),
                  pl.BlockSpec((1, 1, 6 * D), lambda b, i: (b, 0, 0))] + [full(a) for a in consts],
        out_specs=[tok(D), tok(D), pl.BlockSpec((1, 2 * P_HEADS, N_KEYS, tm), lambda b, i: (b, 0, 0, i))],
        out_shape=[jax.ShapeDtypeStruct((B, S, D), F32), jax.ShapeDtypeStruct((B, S, D), BF16),
                   jax.ShapeDtypeStruct((B, 2 * P_HEADS, N_KEYS, S), F32)],
        compiler_params=_params(("parallel", "parallel")),
        name="mixout",
    )(x, ya, yb, ga, gb, mod, *consts)


def _extract_topk(s, key_iota, break_ties):
    rank = jnp.full(s.shape, float(P_TOPK), F32)
    vals = []
    for a in range(P_TOPK):
        m = jnp.max(s, axis=0, keepdims=True)
        hit = s == m
        if break_ties:
            idx = jnp.min(jnp.where(hit, key_iota, float(N_KEYS)), axis=0, keepdims=True)
            hit = key_iota == idx
        rank = jnp.where(hit, float(a), rank)
        s = jnp.where(hit, -jnp.inf, s)
        vals.append(m)
    return vals, rank


def _count(mask):
    return jnp.sum(jnp.where(mask, 1.0, 0.0), axis=0, keepdims=True)


def _pair_tiles(v1, v2, b8):
    v2lo = jnp.concatenate(v2[:8], axis=0)
    v2hi = jnp.concatenate(v2[8:], axis=0)
    v1hi = jnp.concatenate(v1[8:], axis=0)
    tiles = [v1[0] + v2lo, v1[0] + v2hi, v1[1] + v2lo]
    flats = [b8, b8 + 8.0, b8 + float(P_TOPK)]
    valid = [None, None, None]
    for a in range(2, 8):
        ok = b8 < float(P_TOPK // (a + 1))
        tiles.append(jnp.where(ok, v1[a] + v2lo, -jnp.inf))
        flats.append(b8 + float(a * P_TOPK))
        valid.append(ok)
    tiles.append(v1hi + v2[0])
    flats.append((b8 + 8.0) * float(P_TOPK))
    valid.append(None)
    return tiles, flats, valid


def _pair_step(tiles, flats, break_ties):
    m = tiles[0]
    for t in tiles[1:]:
        m = jnp.maximum(m, t)
    m = jnp.max(m, axis=0, keepdims=True)
    if break_ties:
        big = float(P_TOPK * P_TOPK)
        idx = None
        for t, f in zip(tiles, flats):
            cur = jnp.where(t == m, f, big)
            idx = cur if idx is None else jnp.minimum(idx, cur)
        idx = jnp.min(idx, axis=0, keepdims=True)
        tiles = [jnp.where(f == idx, -jnp.inf, t) for t, f in zip(tiles, flats)]
    else:
        tiles = [jnp.where(t == m, -jnp.inf, t) for t in tiles]
    return tiles, m


def _pair_counts(tiles, valid):
    taken = [t == -jnp.inf if ok is None else (t == -jnp.inf) & ok for t, ok in zip(tiles, valid)]
    nb = [_count(taken[0]) + _count(taken[1]), _count(taken[2])] + [_count(taken[a + 1]) for a in range(2, 8)]
    last = jnp.where(taken[9], 1.0, 0.0)
    nb += [last[a:a + 1] for a in range(8)]
    total = nb[0]
    for x in nb[1:]:
        total = total + x
    return nb, total


def _key_step(s):
    m = jnp.max(s, axis=0, keepdims=True)
    return jnp.where(s == m, -jnp.inf, s), m


def _route_fast(sc_ref, e1_ref, nb_ref, e2_ref, r2_ref, b8):
    tl = sc_ref.shape[3]
    want = float(P_TOPK)
    bad = jnp.zeros((1, tl), jnp.bool_)
    s1, s2, v1, v2 = sc_ref[0, 0], sc_ref[0, 1], [], []
    for a in range(P_TOPK):
        s1, m1 = _key_step(s1)
        s2, m2 = _key_step(s2)
        v1.append(m1)
        v2.append(m2)
    for hd in range(P_HEADS):
        tiles, flats, valid = _pair_tiles(v1, v2, b8)
        more = hd + 1 < P_HEADS
        if more:
            t1, t2, w1, w2 = sc_ref[0, 2 * hd + 2], sc_ref[0, 2 * hd + 3], [], []
        m0 = None
        z = jnp.zeros((1, tl), F32)
        for k in range(P_TOPK):
            tiles, m = _pair_step(tiles, flats, False)
            if m0 is None:
                m0 = m
            z = z + jnp.exp(m - m0)
            if more:
                t1, m1 = _key_step(t1)
                t2, m2 = _key_step(t2)
                w1.append(m1)
                w2.append(m2)
        nb, total = _pair_counts(tiles, valid)
        bad = bad | (_count(s1 == -jnp.inf) != want) | (_count(s2 == -jnp.inf) != want) | (total != want)
        o1 = sc_ref[0, 2 * hd]
        o2 = sc_ref[0, 2 * hd + 1]
        nb1 = jnp.zeros((N_KEYS, tl), F32)
        rank2 = jnp.zeros((N_KEYS, tl), F32)
        for a in range(P_TOPK):
            nb1 = jnp.where(o1 == v1[a], nb[a], nb1)
            rank2 = rank2 + jnp.where(v2[a] > o2, 1.0, 0.0)
        e1_ref[0, hd] = jnp.exp(o1 - v1[0]) / z
        nb_ref[0, hd] = nb1
        e2_ref[0, hd] = jnp.exp(o2 - v2[0]).astype(BF16)
        r2_ref[0, hd] = rank2.astype(BF16)
        if more:
            s1, s2, v1, v2 = t1, t2, w1, w2
    return bad


def _route_head(s1, s2, key_iota, b8, break_ties):
    tl = s1.shape[1]
    v1, rank1 = _extract_topk(s1, key_iota, break_ties)
    v2, rank2 = _extract_topk(s2, key_iota, break_ties)
    tiles, flats, valid = _pair_tiles(v1, v2, b8)
    m0 = None
    z = jnp.zeros((1, tl), F32)
    for k in range(P_TOPK):
        tiles, m = _pair_step(tiles, flats, break_ties)
        if m0 is None:
            m0 = m
        z = z + jnp.exp(m - m0)
    nb, total = _pair_counts(tiles, valid)
    nb1 = jnp.zeros((N_KEYS, tl), F32)
    for a in range(P_TOPK):
        nb1 = jnp.where(rank1 == float(a), nb[a], nb1)
    want = float(P_TOPK)
    bad = (_count(rank1 < want) != want) | (_count(rank2 < want) != want) | (total != want)
    e1 = jnp.exp(s1 - v1[0]) / z
    e2 = jnp.exp(s2 - v2[0])
    return e1, nb1, e2, rank2, bad


def _route_kernel(sc_ref, e1_ref, nb_ref, e2_ref, r2_ref):
    tl = sc_ref.shape[3]
    key_iota = lax.broadcasted_iota(jnp.int32, (N_KEYS, tl), 0).astype(F32)
    b8 = lax.broadcasted_iota(jnp.int32, (8, tl), 0).astype(F32)

    bad = _route_fast(sc_ref, e1_ref, nb_ref, e2_ref, r2_ref, b8)

    @pl.when(jnp.max(jnp.where(bad, 1.0, 0.0)) > 0.0)
    def _():
        def head(hd, carry):
            e1, nb1, e2, rank2, _ = _route_head(sc_ref[0, 2 * hd], sc_ref[0, 2 * hd + 1], key_iota, b8, True)
            e1_ref[0, hd] = e1
            nb_ref[0, hd] = nb1
            e2_ref[0, hd] = e2.astype(BF16)
            r2_ref[0, hd] = rank2.astype(BF16)
            return carry

        lax.fori_loop(0, P_HEADS, head, 0)


def _route(scores):
    B, _, _, S = scores.shape
    tl = TL_ROUTE
    spec = pl.BlockSpec((1, P_HEADS, N_KEYS, tl), lambda b, i: (b, 0, 0, i))
    shp = lambda dt: jax.ShapeDtypeStruct((B, P_HEADS, N_KEYS, S), dt)
    return pl.pallas_call(
        _route_kernel,
        grid=(B, S // tl),
        in_specs=[pl.BlockSpec((1, 2 * P_HEADS, N_KEYS, tl), lambda b, i: (b, 0, 0, i))],
        out_specs=[spec] * 4,
        out_shape=[shp(F32), shp(F32), shp(BF16), shp(BF16)],
        compiler_params=_params(("parallel", "parallel")),
        name="route",
    )(scores)


def _peer_kernel(h_ref, e1_ref, nb_ref, e2_ref, r2_ref, u_ref, vt_ref, x1_ref, mod_ref, y_ref,
                 acc_ref, bc_ref, mt_ref, a_ref):
    e = pl.program_id(2)
    rows_sub = SUB_PEER // N_KEYS
    tm = h_ref.shape[1]

    @pl.when(e == 0)
    def _():
        acc_ref[...] = jnp.zeros_like(acc_ref)

    h = h_ref[0]
    zero = jnp.zeros((BF16_ROWS, tm), BF16)
    nsub = TE_PEER // SUB_PEER

    def scores(k):
        return _dot_nt(u_ref[k * SUB_PEER:(k + 1) * SUB_PEER, :], h)

    a_ref[0] = scores(0)
    for k in range(nsub):
        if k >= 1:
            acc_ref[...] += _dot(vt_ref[:, (k - 1) * SUB_PEER:k * SUB_PEER], mt_ref[k - 1])
        if k + 1 < nsub:
            a_ref[(k + 1) % 2] = scores(k + 1)

        for hd in range(P_HEADS):
            for r in range(rows_sub):
                idx = 2 * (hd * rows_sub + r)
                row = k * rows_sub + r
                bc_ref[k, idx] = jnp.broadcast_to(nb_ref[0, hd, row:row + 1, :], (BF16_ROWS, tm)).astype(BF16)
                bc_ref[k, idx + 1] = jnp.broadcast_to(e1_ref[0, hd, row:row + 1, :], (BF16_ROWS, tm)).astype(BF16)

        for ib in range(N_KEYS // BF16_ROWS):
            lo = ib * BF16_ROWS
            w = [None] * rows_sub
            for hd in range(P_HEADS):
                r2 = r2_ref[0, hd, lo:lo + BF16_ROWS, :]
                e2 = e2_ref[0, hd, lo:lo + BF16_ROWS, :]
                for r in range(rows_sub):
                    idx = 2 * (hd * rows_sub + r)
                    t = jnp.where(r2 < bc_ref[k, idx], e2, zero) * bc_ref[k, idx + 1]
                    w[r] = t if w[r] is None else w[r] + t
            for r in range(rows_sub):
                ar = a_ref[k % 2, r * N_KEYS + lo:r * N_KEYS + lo + BF16_ROWS, :]
                g = 0.5 * ar * (1.0 + lax.erf(ar * (1.0 / math.sqrt(2.0))))
                mt_ref[k, r * N_KEYS + lo:r * N_KEYS + lo + BF16_ROWS, :] = w[r] * g.astype(BF16)
    acc_ref[...] += _dot(vt_ref[:, (nsub - 1) * SUB_PEER:nsub * SUB_PEER], mt_ref[nsub - 1])

    @pl.when(e == pl.num_programs(2) - 1)
    def _():
        g2 = mod_ref[0, :, 5 * D:6 * D]
        y_ref[0] = x1_ref[0] + g2 * acc_ref[...].T


def _peer(h2, e1, nb, e2, r2, x1, mod, lw):
    B, S, _ = h2.shape
    tm, te = TM_PEER, TE_PEER
    rows = te // N_KEYS
    tok = pl.BlockSpec((1, tm, D), lambda b, i, e: (b, i, 0))
    slab = pl.BlockSpec((1, P_HEADS, rows, tm), lambda b, i, e: (b, 0, e, i))
    dense = pl.BlockSpec((1, P_HEADS, N_KEYS, tm), lambda b, i, e: (b, 0, 0, i))
    return pl.pallas_call(
        _peer_kernel,
        grid=(B, S // tm, N_EXPERTS // te),
        in_specs=[tok, slab, slab, dense, dense,
                  pl.BlockSpec((te, D), lambda b, i, e: (e, 0)),
                  pl.BlockSpec((D, te), lambda b, i, e: (0, e)),
                  tok, pl.BlockSpec((1, 1, 6 * D), lambda b, i, e: (b, 0, 0))],
        out_specs=tok,
        out_shape=jax.ShapeDtypeStruct((B, S, D), F32),
        scratch_shapes=[pltpu.VMEM((D, tm), F32),
                        pltpu.VMEM((te // SUB_PEER, 2 * P_HEADS * (SUB_PEER // N_KEYS), BF16_ROWS, tm), BF16),
                        pltpu.VMEM((te // SUB_PEER, SUB_PEER, tm), BF16),
                        pltpu.VMEM((2, SUB_PEER, tm), F32)],
        compiler_params=_params(("parallel", "parallel", "arbitrary")),
        name="peer",
    )(h2, e1, nb, e2, r2, lw["peer_u"], lw["peer_vt"], x1, mod)


def _block_ones(n, blk):
    i = jnp.arange(n) // blk
    return (i[:, None] == i[None, :]).astype(BF16)


def _pad_cols(w, n):
    return jnp.concatenate([w, jnp.zeros((w.shape[0], n - w.shape[1]), w.dtype)], axis=1)


def _layer_weights(l, norm1_g, norm2_g, w_in, a_q_norm, a_k_norm, cq_norm, ckv_norm, w_uq, w_ukv,
                   b_q_norm, b_k_norm, w_pa, w_pb, w_out, peer_wq, peer_k1, peer_k2, peer_u, peer_v):
    w = w_in[l]
    k0, k1 = w[:, 512:576], w[:, 576:640]
    v0, v1 = w[:, 640:704], w[:, 704:768]
    z64 = jnp.zeros((D, B_NOPE), F32)
    z32 = jnp.zeros((D, LANES - B_QK), F32)
    w_in_r = jnp.concatenate([w[:, 0:512], k0, k0, k1, k1, v0, v0, v1, v1, w[:, 768:1024], w[:, 1024:1280],
                              w[:, 1312:2336], w[:, 2336:3360], z64, w[:, 1280:1312], z32], axis=1).astype(BF16)
    uq = w_uq[l].reshape(Q_LORA, B_HEADS, B_QK)
    w_uq_r = jnp.pad(uq, ((0, 0), (0, 0), (0, LANES - B_QK))).reshape(Q_LORA, B_HEADS * LANES).astype(BF16)
    ukv = w_ukv[l].reshape(KV_LORA, B_HEADS, B_NOPE + B_V)
    w_uk_r = jnp.pad(ukv[:, :, :B_NOPE], ((0, 0), (0, 0), (0, LANES - B_NOPE))).reshape(KV_LORA, B_HEADS * LANES).astype(BF16)
    w_uv_r = ukv[:, :, B_NOPE:].reshape(KV_LORA, B_HEADS * B_V).astype(BF16)
    row = lambda a: a.reshape(1, -1)
    return {
        "norm1_g": row(norm1_g[l]), "norm2_g": row(norm2_g[l]), "w_in": w_in_r,
        "a_qg": row(jnp.tile(a_q_norm[l], A_HEADS)), "a_kg": row(jnp.tile(a_k_norm[l], 2 * A_KV)),
        "ones_q": _block_ones(A_HEADS * A_DH, A_DH), "ones_k": _block_ones(2 * A_KV * A_DH, A_DH),
        "cq_g": row(cq_norm[l]), "ckv_g": row(ckv_norm[l]),
        "w_uq": w_uq_r, "w_uk": w_uk_r, "w_uv": w_uv_r,
        "b_qg": row(_pad_cols(b_q_norm[l][None], LANES)), "b_kg": row(_pad_cols(b_k_norm[l][None], LANES)),
        "w_pa": w_pa[l].astype(BF16), "w_pb": w_pb[l].astype(BF16), "w_out": w_out[l].astype(BF16),
        "peer_wq": peer_wq[l].astype(BF16), "peer_k1": peer_k1[l].astype(BF16), "peer_k2": peer_k2[l].astype(BF16),
        "peer_u": peer_u[l].astype(BF16), "peer_vt": peer_v[l].astype(BF16).T,
    }


def _rope_tables(S):
    half = B_ROPE // 2
    inv = jnp.power(jnp.float32(ROPE_BASE), -jnp.arange(half, dtype=F32) / half)
    ang = jnp.arange(S, dtype=F32)[:, None] * inv[None, :]
    cos, sin = jnp.cos(ang), jnp.sin(ang)
    zeros = jnp.zeros((S, half), F32)
    pad = jnp.zeros((S, LANES - B_QK), F32)
    cos_t = jnp.concatenate([jnp.ones((S, B_NOPE), F32), cos, cos, pad + 1.0], axis=1)
    sin_lo = jnp.concatenate([jnp.zeros((S, B_NOPE), F32), -sin, zeros, pad], axis=1)
    sin_hi = jnp.concatenate([jnp.zeros((S, B_NOPE), F32), zeros, sin, pad], axis=1)
    return cos_t, sin_lo, sin_hi


def _trunk(x, mods, layers, sinks):
    rope = _rope_tables(x.shape[1])
    for l in range(DEPTH):
        lw = layers[l]
        mod = mods[l]
        qa, ka, va, qb, kb, vb, ga, gb = _inproj(x, mod, lw, rope)
        ya = _attn_a(qa, ka, va, sinks[l])
        yb = _attn_b(qb, kb, vb)
        x1, h2, scores = _mixout(x, ya, yb, ga, gb, mod, lw)
        e1, nb, e2, r2 = _route(scores)
        x = _peer(h2, e1, nb, e2, r2, x1, mod, lw)
    return x


def kernel(x_prompt, x_sample, c_prompt, c_sample, ada_w, ada_b, norm1_g, norm2_g, w_in, a_q_norm, a_k_norm, a_sink, cq_norm, ckv_norm, w_uq, w_ukv, b_q_norm, b_k_norm, w_pa, w_pb, w_out, peer_wq, peer_k1, peer_k2, peer_u, peer_v):
    nb_prompt = c_prompt.shape[0]
    mod_all = _ada(jnp.concatenate([c_prompt, c_sample], axis=0), ada_w, ada_b)
    mods_p = [mod_all[l, :nb_prompt].reshape(nb_prompt, 1, 6 * D) for l in range(DEPTH)]
    mods_s = [mod_all[l, nb_prompt:].reshape(-1, 1, 6 * D) for l in range(DEPTH)]
    layers = [_layer_weights(l, norm1_g, norm2_g, w_in, a_q_norm, a_k_norm, cq_norm, ckv_norm, w_uq, w_ukv,
                             b_q_norm, b_k_norm, w_pa, w_pb, w_out, peer_wq, peer_k1, peer_k2, peer_u, peer_v)
              for l in range(DEPTH)]
    sinks = [a_sink[l] for l in range(DEPTH)]
    y_prompt = _trunk(x_prompt, mods_p, layers, sinks)
    y_sample = _trunk(x_sample, mods_s, layers, sinks)
    return (y_prompt, y_sample)
```

```python
import functools
import math

import jax
import jax.numpy as jnp
from jax import lax
from jax.experimental import pallas as pl
from jax.experimental.pallas import tpu as pltpu

F32 = jnp.float32
BF16 = jnp.bfloat16

D = 1024
DEPTH = 2
EPS = 1e-6
NEG_INF = -1e30
LOG2E = 1.4426950408889634
LANES = 128
BF16_ROWS = 16
BLOCK = 128
A_HEADS, A_KV, A_DH = 8, 2, 64
WINDOW = 128
B_HEADS, B_NOPE, B_ROPE, B_V = 8, 64, 32, 64
B_QK = B_NOPE + B_ROPE
Q_LORA = KV_LORA = 256
ROPE_BASE = 10000.0
P_HEADS, P_HALF, N_KEYS, P_TOPK = 8, 128, 128, 16
N_EXPERTS = N_KEYS * N_KEYS

C_QA, C_KA, C_VA, C_CQ, C_CKV, C_GA, C_GB, C_KR, N_IN = 0, 512, 768, 1024, 1280, 1536, 2560, 3584, 3712

VMEM_LIMIT = 56 * 1024 * 1024

TM_IN = 512
TM_MIX = 512
TQ_B = 512
TK_B = 512
TL_ROUTE = 128
TM_PEER = 512
TE_PEER = 2048
SUB_PEER = 512


def _dot(a, b):
    return jnp.dot(a, b, preferred_element_type=F32)


def _dot_nt(a, b):
    return lax.dot_general(a, b, (((1,), (1,)), ((), ())), preferred_element_type=F32)


def _params(sem):
    return pltpu.CompilerParams(dimension_semantics=sem, vmem_limit_bytes=VMEM_LIMIT)


def _ada_kernel(c_ref, w_ref, b_ref, o_ref):
    c = c_ref[...]
    s = c * jax.nn.sigmoid(c)
    w = w_ref[0]
    s_hi = s.astype(BF16)
    s_lo = (s - s_hi.astype(F32)).astype(BF16)
    w_hi = w.astype(BF16)
    w_lo = (w - w_hi.astype(F32)).astype(BF16)
    o_ref[0] = _dot(s_hi, w_hi) + _dot(s_lo, w_hi) + _dot(s_hi, w_lo) + b_ref[0]


def _ada(c_all, ada_w, ada_b):
    n = c_all.shape[0]
    tn = 1536
    return pl.pallas_call(
        _ada_kernel,
        grid=(DEPTH, 6 * D // tn),
        in_specs=[
            pl.BlockSpec((n, D), lambda l, j: (0, 0)),
            pl.BlockSpec((1, D, tn), lambda l, j: (l, 0, j)),
            pl.BlockSpec((1, 1, tn), lambda l, j: (l, 0, j)),
        ],
        out_specs=pl.BlockSpec((1, n, tn), lambda l, j: (l, 0, j)),
        out_shape=jax.ShapeDtypeStruct((DEPTH, n, 6 * D), F32),
        compiler_params=_params(("arbitrary", "arbitrary")),
        name="ada",
    )(c_all, ada_w, ada_b.reshape(DEPTH, 1, 6 * D))


def _rope_tile(x, cos, sin_lo, sin_hi):
    return x * cos + pltpu.roll(x, LANES - B_ROPE // 2, 1) * sin_lo + pltpu.roll(x, B_ROPE // 2, 1) * sin_hi


def _inproj_kernel(x_ref, mod_ref, g1_ref, win_ref, aqg_ref, akg_ref, onesq_ref, onesk_ref,
                   cqg_ref, ckvg_ref, wuq_ref, wuk_ref, wuv_ref, bqg_ref, bkg_ref,
                   cos_ref, sinlo_ref, sinhi_ref,
                   qa_ref, ka_ref, va_ref, qb_ref, kb_ref, vb_ref, ga_ref, gb_ref):
    x = x_ref[0]
    sh1 = mod_ref[0, :, 0:D]
    sc1 = mod_ref[0, :, D:2 * D]
    ms = jnp.mean(x * x, axis=-1, keepdims=True)
    h = (x * lax.rsqrt(ms + EPS)) * g1_ref[...]
    h = h * (1.0 + sc1) + sh1
    hb = h.astype(BF16)

    def proj(a, b):
        return _dot(hb, win_ref[:, a:b])

    z = proj(C_QA, C_KA)
    ss = _dot((z * z).astype(BF16), onesq_ref[...])
    qa_ref[0] = (z * lax.rsqrt(ss * (1.0 / A_DH) + EPS) * aqg_ref[...] * (A_DH ** -0.5)).astype(BF16)
    z = proj(C_KA, C_VA)
    ss = _dot((z * z).astype(BF16), onesk_ref[...])
    ka_ref[0] = (z * lax.rsqrt(ss * (1.0 / A_DH) + EPS) * akg_ref[...]).astype(BF16)
    va_ref[0] = proj(C_VA, C_CQ).astype(BF16)

    cos = cos_ref[...]
    sin_lo = sinlo_ref[...]
    sin_hi = sinhi_ref[...]

    z = proj(C_CQ, C_CKV)
    ms = jnp.mean(z * z, axis=-1, keepdims=True)
    cqn = (z * lax.rsqrt(ms + EPS) * cqg_ref[...]).astype(BF16)
    q = _dot(cqn, wuq_ref[...])
    heads = range(B_HEADS)
    qs = [q[:, hd * LANES:(hd + 1) * LANES] for hd in heads]
    inv = [lax.rsqrt(jnp.sum(x * x, axis=-1, keepdims=True) * (1.0 / B_QK) + EPS) for x in qs]
    bqg = bqg_ref[...] * (B_QK ** -0.5 * LOG2E)
    qs = [x * r * bqg for x, r in zip(qs, inv)]
    qs = [_rope_tile(x, cos, sin_lo, sin_hi) for x in qs]
    for hd in heads:
        qb_ref[0, :, hd * LANES:(hd + 1) * LANES] = qs[hd].astype(BF16)

    z = proj(C_CKV, C_GA)
    ms = jnp.mean(z * z, axis=-1, keepdims=True)
    ckvn = (z * lax.rsqrt(ms + EPS) * ckvg_ref[...]).astype(BF16)
    kn = _dot(ckvn, wuk_ref[...])
    vb_ref[0] = _dot(ckvn, wuv_ref[...]).astype(BF16)
    kr = proj(C_KR, N_IN)
    bkg = bkg_ref[...]
    ks = [kn[:, hd * LANES:(hd + 1) * LANES] + kr for hd in heads]
    inv = [lax.rsqrt(jnp.sum(x * x, axis=-1, keepdims=True) * (1.0 / B_QK) + EPS) for x in ks]
    ks = [x * r * bkg for x, r in zip(ks, inv)]
    ks = [_rope_tile(x, cos, sin_lo, sin_hi) for x in ks]
    for hd in heads:
        kb_ref[0, :, hd * LANES:(hd + 1) * LANES] = ks[hd].astype(BF16)

    ga_ref[0] = jax.nn.sigmoid(proj(C_GA, C_GB)).astype(BF16)
    gb_ref[0] = jax.nn.sigmoid(proj(C_GB, C_KR)).astype(BF16)


def _inproj(x, mod, lw, rope):
    B, S, _ = x.shape
    tm = TM_IN
    tok = lambda w: pl.BlockSpec((1, tm, w), lambda b, i: (b, i, 0))
    full = lambda a: pl.BlockSpec(a.shape, lambda b, i: (0,) * a.ndim)
    rope_spec = pl.BlockSpec((tm, LANES), lambda b, i: (i, 0))
    consts = [lw["norm1_g"], lw["w_in"], lw["a_qg"], lw["a_kg"], lw["ones_q"], lw["ones_k"],
              lw["cq_g"], lw["ckv_g"], lw["w_uq"], lw["w_uk"], lw["w_uv"], lw["b_qg"], lw["b_kg"]]
    widths = [512, 256, 256, 1024, 1024, 512, 1024, 1024]
    return pl.pallas_call(
        _inproj_kernel,
        grid=(B, S // tm),
        in_specs=[tok(D), pl.BlockSpec((1, 1, 6 * D), lambda b, i: (b, 0, 0))]
                 + [full(a) for a in consts] + [rope_spec] * 3,
        out_specs=[tok(w) for w in widths],
        out_shape=[jax.ShapeDtypeStruct((B, S, w), BF16) for w in widths],
        compiler_params=_params(("parallel", "parallel")),
        name="inproj",
    )(x, mod, *consts, *rope)


def _attn_a_kernel(sink_ref, q_ref, k_ref, v_ref, o_ref, bias_ref):
    n = pl.program_id(1)
    S = k_ref.shape[1]
    span = 3 * BLOCK
    group = A_HEADS // A_KV
    start = pl.multiple_of(jnp.clip((n - 1) * BLOCK, 0, S - span), BLOCK)
    kw = k_ref[0, pl.ds(start, span), :]
    vw = v_ref[0, pl.ds(start, span), :]
    rows = group * BLOCK
    lane = lax.broadcasted_iota(jnp.int32, (BLOCK, LANES), 1)
    rowhead = lax.broadcasted_iota(jnp.int32, (rows, 1), 0) // BLOCK

    @pl.when((n <= 1) | (n == pl.num_programs(1) - 1))
    def _():
        qpos = n * BLOCK + lax.broadcasted_iota(jnp.int32, (rows, span), 0) % BLOCK
        kpos = start + lax.broadcasted_iota(jnp.int32, (rows, span), 1)
        dist = jnp.abs(qpos - kpos)
        distf = dist.astype(F32)
        for j in range(A_KV):
            slope = jnp.exp2(-8.0 * (group * j + rowhead + 1).astype(F32) / A_HEADS)
            bias_ref[j] = jnp.where(dist <= WINDOW, -slope * distf, NEG_INF)

    for j in range(A_KV):
        kd = kw[:, j * LANES:(j + 1) * LANES]
        vd = vw[:, j * LANES:(j + 1) * LANES]
        parts = []
        for pp in range(group // 2):
            p = (group // 2) * j + pp
            qp = q_ref[0, :, p * LANES:(p + 1) * LANES]
            zero = jnp.zeros_like(qp)
            parts.append(jnp.where(lane < A_DH, qp, zero))
            parts.append(jnp.where(lane >= A_DH, qp, zero))
        q4 = jnp.concatenate(parts, axis=0)
        s = _dot_nt(q4, kd)
        sink = jnp.zeros((rows, 1), F32)
        for g in range(group):
            sink = jnp.where(rowhead == g, sink_ref[group * j + g], sink)
        bias = bias_ref[j]
        logits = jnp.where(bias > 0.5 * NEG_INF, s + bias, NEG_INF)
        m = jnp.maximum(jnp.max(logits, axis=-1, keepdims=True), sink)
        p = jnp.exp(logits - m)
        denom = jnp.sum(p, axis=-1, keepdims=True) + jnp.exp(sink - m)
        o = _dot((p / denom).astype(BF16), vd)
        for pp in range(group // 2):
            oe = o[(2 * pp) * BLOCK:(2 * pp + 1) * BLOCK]
            oo = o[(2 * pp + 1) * BLOCK:(2 * pp + 2) * BLOCK]
            p_idx = (group // 2) * j + pp
            o_ref[0, :, p_idx * LANES:(p_idx + 1) * LANES] = jnp.where(lane < A_DH, oe, oo).astype(BF16)


def _attn_a(qa, ka, va, sink):
    B, S, _ = qa.shape
    return pl.pallas_call(
        _attn_a_kernel,
        grid=(B, S // BLOCK),
        in_specs=[
            pl.BlockSpec(memory_space=pltpu.SMEM),
            pl.BlockSpec((1, BLOCK, 512), lambda b, n: (b, n, 0)),
            pl.BlockSpec((1, S, 256), lambda b, n: (b, 0, 0)),
            pl.BlockSpec((1, S, 256), lambda b, n: (b, 0, 0)),
        ],
        out_specs=pl.BlockSpec((1, BLOCK, 512), lambda b, n: (b, n, 0)),
        out_shape=jax.ShapeDtypeStruct((B, S, 512), BF16),
        scratch_shapes=[pltpu.VMEM((A_KV, (A_HEADS // A_KV) * BLOCK, 3 * BLOCK), F32)],
        compiler_params=_params(("arbitrary", "arbitrary")),
        name="attn_a",
    )(sink, qa, ka, va)


def _attn_b_kernel(q_ref, k_ref, v_ref, o_ref, s_ref, p_ref):
    S = k_ref.shape[1]
    tq = q_ref.shape[1]
    nk = S // TK_B
    lane = lax.broadcasted_iota(jnp.int32, (tq, LANES), 1)
    outs = []
    for hh in range(2):
        q = q_ref[0, :, hh * LANES:(hh + 1) * LANES]

        def kchunk(j, hh=hh):
            return k_ref[0, pl.ds(pl.multiple_of(j * TK_B, TK_B), TK_B), hh * LANES:(hh + 1) * LANES]

        def vchunk(j):
            return v_ref[0, pl.ds(pl.multiple_of(j * TK_B, TK_B), TK_B), :]

        def stage(j, slot, carry, q=q, kchunk=kchunk, vchunk=vchunk):
            m, l, alpha_prev, acc = carry
            s_ref[1 - slot] = _dot_nt(q, kchunk(jnp.minimum(j + 1, nk - 1)))
            acc = acc * alpha_prev + _dot(p_ref[1 - slot], vchunk(jnp.maximum(j - 1, 0)))
            m_new = jnp.maximum(m, jnp.max(s_ref[slot], axis=-1, keepdims=True))
            alpha = jnp.exp2(m - m_new)
            p = jnp.exp2(s_ref[slot] - m_new)
            p_ref[slot] = p.astype(BF16)
            psum = p[:, 0:LANES]
            for c in range(1, TK_B // LANES):
                psum = psum + p[:, c * LANES:(c + 1) * LANES]
            return m_new, alpha * l + psum, alpha, acc

        def body(jj, carry, stage=stage):
            carry = stage(2 * jj, 0, carry)
            return stage(2 * jj + 1, 1, carry)

        s_ref[0] = _dot_nt(q, kchunk(0))
        p_ref[1] = jnp.zeros(p_ref.shape[1:], BF16)
        init = (jnp.full((tq, 1), -jnp.inf, F32), jnp.zeros((tq, LANES), F32), jnp.ones((tq, 1), F32),
                jnp.zeros((tq, LANES), F32))
        carry = init
        for jj in range(nk // 2):
            carry = body(jj, carry)
        m, l, alpha, acc = carry
        acc = acc * alpha + _dot(p_ref[1], vchunk(nk - 1))
        outs.append(acc / jnp.sum(l, axis=-1, keepdims=True))
    o_ref[0] = jnp.where(lane < B_V, outs[0], outs[1]).astype(BF16)


def _attn_b(qb, kb, vb):
    B, S, _ = qb.shape
    tq = TQ_B
    return pl.pallas_call(
        _attn_b_kernel,
        grid=(B, B_HEADS // 2, S // tq),
        in_specs=[
            pl.BlockSpec((1, tq, 2 * LANES), lambda b, p, i: (b, i, p)),
            pl.BlockSpec((1, S, 2 * LANES), lambda b, p, i: (b, 0, p)),
            pl.BlockSpec((1, S, LANES), lambda b, p, i: (b, 0, p)),
        ],
        out_specs=pl.BlockSpec((1, tq, LANES), lambda b, p, i: (b, i, p)),
        out_shape=jax.ShapeDtypeStruct((B, S, B_HEADS * B_V), BF16),
        scratch_shapes=[pltpu.VMEM((2, tq, TK_B), F32), pltpu.VMEM((2, tq, TK_B), BF16)],
        compiler_params=_params(("parallel", "parallel", "arbitrary")),
        name="attn_b",
    )(qb, kb, vb)


def _mixout_kernel(x_ref, ya_ref, yb_ref, ga_ref, gb_ref, mod_ref, wpa_ref, wpb_ref, wout_ref,
                   g2_ref, wq_ref, k1_ref, k2_ref, x1_ref, h2_ref, sc_ref):
    ma = _dot(ya_ref[0], wpa_ref[...])
    mb = _dot(yb_ref[0], wpb_ref[...])
    m = ga_ref[0].astype(F32) * ma + gb_ref[0].astype(F32) * mb
    mo = _dot(m.astype(BF16), wout_ref[...])
    g1 = mod_ref[0, :, 2 * D:3 * D]
    sh2 = mod_ref[0, :, 3 * D:4 * D]
    sc2 = mod_ref[0, :, 4 * D:5 * D]
    x1 = x_ref[0] + g1 * mo
    x1_ref[0] = x1
    ms = jnp.mean(x1 * x1, axis=-1, keepdims=True)
    h2 = (x1 * lax.rsqrt(ms + EPS)) * g2_ref[...]
    h2 = (h2 * (1.0 + sc2) + sh2).astype(BF16)
    h2_ref[0] = h2
    q = _dot(h2, wq_ref[...]).astype(BF16)
    for hd in range(P_HEADS):
        for c, kref in enumerate((k1_ref, k2_ref)):
            col = (2 * hd + c) * P_HALF
            sc_ref[0, 2 * hd + c] = _dot_nt(kref[...], q[:, col:col + P_HALF])


def _mixout(x, ya, yb, ga, gb, mod, lw):
    B, S, _ = x.shape
    tm = TM_MIX
    tok = lambda w: pl.BlockSpec((1, tm, w), lambda b, i: (b, i, 0))
    full = lambda a: pl.BlockSpec(a.shape, lambda b, i: (0,) * a.ndim)
    consts = [lw["w_pa"], lw["w_pb"], lw["w_out"], lw["norm2_g"], lw["peer_wq"], lw["peer_k1"], lw["peer_k2"]]
    return pl.pallas_call(
        _mixout_kernel,
        grid=(B, S // tm),
        in_specs=[tok(D), tok(512), tok(512), tok(D), tok(D),
                  pl.BlockSpec((1, 1, 6 * D), lambda b, i: (b, 0, 0))] + [full(a) for a in consts],
        out_specs=[tok(D), tok(D), pl.BlockSpec((1, 2 * P_HEADS, N_KEYS, tm), lambda b, i: (b, 0, 0, i))],
        out_shape=[jax.ShapeDtypeStruct((B, S, D), F32), jax.ShapeDtypeStruct((B, S, D), BF16),
                   jax.ShapeDtypeStruct((B, 2 * P_HEADS, N_KEYS, S), F32)],
        compiler_params=_params(("parallel", "parallel")),
        name="mixout",
    )(x, ya, yb, ga, gb, mod, *consts)


def _extract_topk(s, key_iota, break_ties):
    rank = jnp.full(s.shape, float(P_TOPK), F32)
    vals = []
    for a in range(P_TOPK):
        m = jnp.max(s, axis=0, keepdims=True)
        hit = s == m
        if break_ties:
            idx = jnp.min(jnp.where(hit, key_iota, float(N_KEYS)), axis=0, keepdims=True)
            hit = key_iota == idx
        rank = jnp.where(hit, float(a), rank)
        s = jnp.where(hit, -jnp.inf, s)
        vals.append(m)
    return vals, rank


def _count(mask):
    return jnp.sum(jnp.where(mask, 1.0, 0.0), axis=0, keepdims=True)


def _pair_tiles(v1, v2, b8):
    v2lo = jnp.concatenate(v2[:8], axis=0)
    v2hi = jnp.concatenate(v2[8:], axis=0)
    v1hi = jnp.concatenate(v1[8:], axis=0)
    tiles = [v1[0] + v2lo, v1[0] + v2hi, v1[1] + v2lo]
    flats = [b8, b8 + 8.0, b8 + float(P_TOPK)]
    valid = [None, None, None]
    for a in range(2, 8):
        ok = b8 < float(P_TOPK // (a + 1))
        tiles.append(jnp.where(ok, v1[a] + v2lo, -jnp.inf))
        flats.append(b8 + float(a * P_TOPK))
        valid.append(ok)
    tiles.append(v1hi + v2[0])
    flats.append((b8 + 8.0) * float(P_TOPK))
    valid.append(None)
    return tiles, flats, valid


def _pair_step(tiles, flats, break_ties):
    m = tiles[0]
    for t in tiles[1:]:
        m = jnp.maximum(m, t)
    m = jnp.max(m, axis=0, keepdims=True)
    if break_ties:
        big = float(P_TOPK * P_TOPK)
        idx = None
        for t, f in zip(tiles, flats):
            cur = jnp.where(t == m, f, big)
            idx = cur if idx is None else jnp.minimum(idx, cur)
        idx = jnp.min(idx, axis=0, keepdims=True)
        tiles = [jnp.where(f == idx, -jnp.inf, t) for t, f in zip(tiles, flats)]
    else:
        tiles = [jnp.where(t == m, -jnp.inf, t) for t in tiles]
    return tiles, m


def _pair_counts(tiles, valid):
    taken = [t == -jnp.inf if ok is None else (t == -jnp.inf) & ok for t, ok in zip(tiles, valid)]
    nb = [_count(taken[0]) + _count(taken[1]), _count(taken[2])] + [_count(taken[a + 1]) for a in range(2, 8)]
    last = jnp.where(taken[9], 1.0, 0.0)
    nb += [last[a:a + 1] for a in range(8)]
    total = nb[0]
    for x in nb[1:]:
        total = total + x
    return nb, total


def _key_step(s):
    m = jnp.max(s, axis=0, keepdims=True)
    return jnp.where(s == m, -jnp.inf, s), m


def _route_fast(sc_ref, e1_ref, nb_ref, e2_ref, r2_ref, b8):
    tl = sc_ref.shape[3]
    want = float(P_TOPK)
    bad = jnp.zeros((1, tl), jnp.bool_)
    s1, s2, v1, v2 = sc_ref[0, 0], sc_ref[0, 1], [], []
    for a in range(P_TOPK):
        s1, m1 = _key_step(s1)
        s2, m2 = _key_step(s2)
        v1.append(m1)
        v2.append(m2)
    for hd in range(P_HEADS):
        tiles, flats, valid = _pair_tiles(v1, v2, b8)
        more = hd + 1 < P_HEADS
        if more:
            t1, t2, w1, w2 = sc_ref[0, 2 * hd + 2], sc_ref[0, 2 * hd + 3], [], []
        m0 = None
        z = jnp.zeros((1, tl), F32)
        for k in range(P_TOPK):
            tiles, m = _pair_step(tiles, flats, False)
            if m0 is None:
                m0 = m
            z = z + jnp.exp(m - m0)
            if more:
                t1, m1 = _key_step(t1)
                t2, m2 = _key_step(t2)
                w1.append(m1)
                w2.append(m2)
        nb, total = _pair_counts(tiles, valid)
        bad = bad | (_count(s1 == -jnp.inf) != want) | (_count(s2 == -jnp.inf) != want) | (total != want)
        o1 = sc_ref[0, 2 * hd]
        o2 = sc_ref[0, 2 * hd + 1]
        nb1 = jnp.zeros((N_KEYS, tl), F32)
        rank2 = jnp.zeros((N_KEYS, tl), F32)
        for a in range(P_TOPK):
            nb1 = jnp.where(o1 == v1[a], nb[a], nb1)
            rank2 = jnp.where(v2[a] > o2, float(a + 1), rank2)
        e1_ref[0, hd] = jnp.exp(o1 - v1[0]) / (z + z)
        nb_ref[0, hd] = nb1
        e2_ref[0, hd] = jnp.exp(o2 - v2[0]).astype(BF16)
        r2_ref[0, hd] = rank2.astype(BF16)
        if more:
            s1, s2, v1, v2 = t1, t2, w1, w2
    return bad


def _route_head(s1, s2, key_iota, b8, break_ties):
    tl = s1.shape[1]
    v1, rank1 = _extract_topk(s1, key_iota, break_ties)
    v2, rank2 = _extract_topk(s2, key_iota, break_ties)
    tiles, flats, valid = _pair_tiles(v1, v2, b8)
    m0 = None
    z = jnp.zeros((1, tl), F32)
    for k in range(P_TOPK):
        tiles, m = _pair_step(tiles, flats, break_ties)
        if m0 is None:
            m0 = m
        z = z + jnp.exp(m - m0)
    nb, total = _pair_counts(tiles, valid)
    nb1 = jnp.zeros((N_KEYS, tl), F32)
    for a in range(P_TOPK):
        nb1 = jnp.where(rank1 == float(a), nb[a], nb1)
    want = float(P_TOPK)
    bad = (_count(rank1 < want) != want) | (_count(rank2 < want) != want) | (total != want)
    e1 = jnp.exp(s1 - v1[0]) / (z + z)
    e2 = jnp.exp(s2 - v2[0])
    return e1, nb1, e2, rank2, bad


def _route_kernel(sc_ref, e1_ref, nb_ref, e2_ref, r2_ref):
    tl = sc_ref.shape[3]
    key_iota = lax.broadcasted_iota(jnp.int32, (N_KEYS, tl), 0).astype(F32)
    b8 = lax.broadcasted_iota(jnp.int32, (8, tl), 0).astype(F32)

    bad = _route_fast(sc_ref, e1_ref, nb_ref, e2_ref, r2_ref, b8)

    @pl.when(jnp.max(jnp.where(bad, 1.0, 0.0)) > 0.0)
    def _():
        def head(hd, carry):
            e1, nb1, e2, rank2, _ = _route_head(sc_ref[0, 2 * hd], sc_ref[0, 2 * hd + 1], key_iota, b8, True)
            e1_ref[0, hd] = e1
            nb_ref[0, hd] = nb1
            e2_ref[0, hd] = e2.astype(BF16)
            r2_ref[0, hd] = rank2.astype(BF16)
            return carry

        lax.fori_loop(0, P_HEADS, head, 0)


def _route(scores):
    B, _, _, S = scores.shape
    tl = TL_ROUTE
    spec = pl.BlockSpec((1, P_HEADS, N_KEYS, tl), lambda b, i: (b, 0, 0, i))
    shp = lambda dt: jax.ShapeDtypeStruct((B, P_HEADS, N_KEYS, S), dt)
    return pl.pallas_call(
        _route_kernel,
        grid=(B, S // tl),
        in_specs=[pl.BlockSpec((1, 2 * P_HEADS, N_KEYS, tl), lambda b, i: (b, 0, 0, i))],
        out_specs=[spec] * 4,
        out_shape=[shp(F32), shp(F32), shp(BF16), shp(BF16)],
        compiler_params=_params(("parallel", "parallel")),
        name="route",
    )(scores)


def _peer_kernel(h_ref, e1_ref, nb_ref, e2_ref, r2_ref, u_ref, vt_ref, x1_ref, mod_ref, y_ref,
                 acc_ref, bc_ref, mt_ref, a_ref):
    e = pl.program_id(2)
    rows_sub = SUB_PEER // N_KEYS
    tm = h_ref.shape[1]

    @pl.when(e == 0)
    def _():
        acc_ref[...] = jnp.zeros_like(acc_ref)

    h = h_ref[0]
    zero = jnp.zeros((BF16_ROWS, tm), BF16)
    nsub = TE_PEER // SUB_PEER

    def scores(k):
        return _dot_nt(u_ref[k * SUB_PEER:(k + 1) * SUB_PEER, :], h)

    a_ref[0] = scores(0)
    for k in range(nsub):
        if k >= 1:
            acc_ref[...] += _dot(vt_ref[:, (k - 1) * SUB_PEER:k * SUB_PEER], mt_ref[k - 1])
        if k + 1 < nsub:
            a_ref[(k + 1) % 2] = scores(k + 1)

        for hd in range(P_HEADS):
            for r in range(rows_sub):
                idx = 2 * (hd * rows_sub + r)
                row = k * rows_sub + r
                bc_ref[k, idx] = jnp.broadcast_to(nb_ref[0, hd, row:row + 1, :], (BF16_ROWS, tm)).astype(BF16)
                bc_ref[k, idx + 1] = jnp.broadcast_to(e1_ref[0, hd, row:row + 1, :], (BF16_ROWS, tm)).astype(BF16)

        for ib in range(N_KEYS // BF16_ROWS):
            lo = ib * BF16_ROWS
            w = [None] * rows_sub
            for hd in range(P_HEADS):
                r2 = r2_ref[0, hd, lo:lo + BF16_ROWS, :]
                e2 = e2_ref[0, hd, lo:lo + BF16_ROWS, :]
                for r in range(rows_sub):
                    idx = 2 * (hd * rows_sub + r)
                    t = jnp.where(r2 < bc_ref[k, idx], e2, zero) * bc_ref[k, idx + 1]
                    w[r] = t if w[r] is None else w[r] + t
            for r in range(rows_sub):
                ar = a_ref[k % 2, r * N_KEYS + lo:r * N_KEYS + lo + BF16_ROWS, :]
                g = ar * (1.0 + lax.erf(ar * (1.0 / math.sqrt(2.0))))
                mt_ref[k, r * N_KEYS + lo:r * N_KEYS + lo + BF16_ROWS, :] = w[r] * g.astype(BF16)
    acc_ref[...] += _dot(vt_ref[:, (nsub - 1) * SUB_PEER:nsub * SUB_PEER], mt_ref[nsub - 1])

    @pl.when(e == pl.num_programs(2) - 1)
    def _():
        g2 = mod_ref[0, :, 5 * D:6 * D]
        y_ref[0] = x1_ref[0] + g2 * acc_ref[...].T


def _peer(h2, e1, nb, e2, r2, x1, mod, lw):
    B, S, _ = h2.shape
    tm, te = TM_PEER, TE_PEER
    rows = te // N_KEYS
    tok = pl.BlockSpec((1, tm, D), lambda b, i, e: (b, i, 0))
    slab = pl.BlockSpec((1, P_HEADS, rows, tm), lambda b, i, e: (b, 0, e, i))
    dense = pl.BlockSpec((1, P_HEADS, N_KEYS, tm), lambda b, i, e: (b, 0, 0, i))
    return pl.pallas_call(
        _peer_kernel,
        grid=(B, S // tm, N_EXPERTS // te),
        in_specs=[tok, slab, slab, dense, dense,
                  pl.BlockSpec((te, D), lambda b, i, e: (e, 0)),
                  pl.BlockSpec((D, te), lambda b, i, e: (0, e)),
                  tok, pl.BlockSpec((1, 1, 6 * D), lambda b, i, e: (b, 0, 0))],
        out_specs=tok,
        out_shape=jax.ShapeDtypeStruct((B, S, D), F32),
        scratch_shapes=[pltpu.VMEM((D, tm), F32),
                        pltpu.VMEM((te // SUB_PEER, 2 * P_HEADS * (SUB_PEER // N_KEYS), BF16_ROWS, tm), BF16),
                        pltpu.VMEM((te // SUB_PEER, SUB_PEER, tm), BF16),
                        pltpu.VMEM((2, SUB_PEER, tm), F32)],
        compiler_params=_params(("parallel", "parallel", "arbitrary")),
        name="peer",
    )(h2, e1, nb, e2, r2, lw["peer_u"], lw["peer_vt"], x1, mod)


def _block_ones(n, blk):
    i = jnp.arange(n) // blk
    return (i[:, None] == i[None, :]).astype(BF16)


def _pad_cols(w, n):
    return jnp.concatenate([w, jnp.zeros((w.shape[0], n - w.shape[1]), w.dtype)], axis=1)


def _layer_weights(l, norm1_g, norm2_g, w_in, a_q_norm, a_k_norm, cq_norm, ckv_norm, w_uq, w_ukv,
                   b_q_norm, b_k_norm, w_pa, w_pb, w_out, peer_wq, peer_k1, peer_k2, peer_u, peer_v):
    w = w_in[l]
    k0, k1 = w[:, 512:576], w[:, 576:640]
    v0, v1 = w[:, 640:704], w[:, 704:768]
    z64 = jnp.zeros((D, B_NOPE), F32)
    z32 = jnp.zeros((D, LANES - B_QK), F32)
    w_in_r = jnp.concatenate([w[:, 0:512], k0, k0, k1, k1, v0, v0, v1, v1, w[:, 768:1024], w[:, 1024:1280],
                              w[:, 1312:2336], w[:, 2336:3360], z64, w[:, 1280:1312], z32], axis=1).astype(BF16)
    uq = w_uq[l].reshape(Q_LORA, B_HEADS, B_QK)
    w_uq_r = jnp.pad(uq, ((0, 0), (0, 0), (0, LANES - B_QK))).reshape(Q_LORA, B_HEADS * LANES).astype(BF16)
    ukv = w_ukv[l].reshape(KV_LORA, B_HEADS, B_NOPE + B_V)
    w_uk_r = jnp.pad(ukv[:, :, :B_NOPE], ((0, 0), (0, 0), (0, LANES - B_NOPE))).reshape(KV_LORA, B_HEADS * LANES).astype(BF16)
    w_uv_r = ukv[:, :, B_NOPE:].reshape(KV_LORA, B_HEADS * B_V).astype(BF16)
    row = lambda a: a.reshape(1, -1)
    return {
        "norm1_g": row(norm1_g[l]), "norm2_g": row(norm2_g[l]), "w_in": w_in_r,
        "a_qg": row(jnp.tile(a_q_norm[l], A_HEADS)), "a_kg": row(jnp.tile(a_k_norm[l], 2 * A_KV)),
        "ones_q": _block_ones(A_HEADS * A_DH, A_DH), "ones_k": _block_ones(2 * A_KV * A_DH, A_DH),
        "cq_g": row(cq_norm[l]), "ckv_g": row(ckv_norm[l]),
        "w_uq": w_uq_r, "w_uk": w_uk_r, "w_uv": w_uv_r,
        "b_qg": row(_pad_cols(b_q_norm[l][None], LANES)), "b_kg": row(_pad_cols(b_k_norm[l][None], LANES)),
        "w_pa": w_pa[l].astype(BF16), "w_pb": w_pb[l].astype(BF16), "w_out": w_out[l].astype(BF16),
        "peer_wq": peer_wq[l].astype(BF16), "peer_k1": peer_k1[l].astype(BF16), "peer_k2": peer_k2[l].astype(BF16),
        "peer_u": peer_u[l].astype(BF16), "peer_vt": peer_v[l].astype(BF16).T,
    }


def _rope_tables(S):
    half = B_ROPE // 2
    inv = jnp.power(jnp.float32(ROPE_BASE), -jnp.arange(half, dtype=F32) / half)
    ang = jnp.arange(S, dtype=F32)[:, None] * inv[None, :]
    cos, sin = jnp.cos(ang), jnp.sin(ang)
    zeros = jnp.zeros((S, half), F32)
    pad = jnp.zeros((S, LANES - B_QK), F32)
    cos_t = jnp.concatenate([jnp.ones((S, B_NOPE), F32), cos, cos, pad + 1.0], axis=1)
    sin_lo = jnp.concatenate([jnp.zeros((S, B_NOPE), F32), -sin, zeros, pad], axis=1)
    sin_hi = jnp.concatenate([jnp.zeros((S, B_NOPE), F32), zeros, sin, pad], axis=1)
    return cos_t, sin_lo, sin_hi


def _trunk(x, mods, layers, sinks):
    rope = _rope_tables(x.shape[1])
    for l in range(DEPTH):
        lw = layers[l]
        mod = mods[l]
        qa, ka, va, qb, kb, vb, ga, gb = _inproj(x, mod, lw, rope)
        ya = _attn_a(qa, ka, va, sinks[l])
        yb = _attn_b(qb, kb, vb)
        x1, h2, scores = _mixout(x, ya, yb, ga, gb, mod, lw)
        e1, nb, e2, r2 = _route(scores)
        x = _peer(h2, e1, nb, e2, r2, x1, mod, lw)
    return x


def kernel(x_prompt, x_sample, c_prompt, c_sample, ada_w, ada_b, norm1_g, norm2_g, w_in, a_q_norm, a_k_norm, a_sink, cq_norm, ckv_norm, w_uq, w_ukv, b_q_norm, b_k_norm, w_pa, w_pb, w_out, peer_wq, peer_k1, peer_k2, peer_u, peer_v):
    nb_prompt = c_prompt.shape[0]
    mod_all = _ada(jnp.concatenate([c_prompt, c_sample], axis=0), ada_w, ada_b)
    mods_p = [mod_all[l, :nb_prompt].reshape(nb_prompt, 1, 6 * D) for l in range(DEPTH)]
    mods_s = [mod_all[l, nb_prompt:].reshape(-1, 1, 6 * D) for l in range(DEPTH)]
    layers = [_layer_weights(l, norm1_g, norm2_g, w_in, a_q_norm, a_k_norm, cq_norm, ckv_norm, w_uq, w_ukv,
                             b_q_norm, b_k_norm, w_pa, w_pb, w_out, peer_wq, peer_k1, peer_k2, peer_u, peer_v)
              for l in range(DEPTH)]
    sinks = [a_sink[l] for l in range(DEPTH)]
    y_prompt = _trunk(x_prompt, mods_p, layers, sinks)
    y_sample = _trunk(x_sample, mods_s, layers, sinks)
    return (y_prompt, y_sample)
```

```python
import functools
import math

import jax
import jax.numpy as jnp
from jax import lax
from jax.experimental import pallas as pl
from jax.experimental.pallas import tpu as pltpu

F32 = jnp.float32
BF16 = jnp.bfloat16

D = 1024
DEPTH = 2
EPS = 1e-6
NEG_INF = -1e30
LOG2E = 1.4426950408889634
LANES = 128
BF16_ROWS = 16
BLOCK = 128
A_HEADS, A_KV, A_DH = 8, 2, 64
WINDOW = 128
B_HEADS, B_NOPE, B_ROPE, B_V = 8, 64, 32, 64
B_QK = B_NOPE + B_ROPE
Q_LORA = KV_LORA = 256
ROPE_BASE = 10000.0
P_HEADS, P_HALF, N_KEYS, P_TOPK = 8, 128, 128, 16
N_EXPERTS = N_KEYS * N_KEYS

C_QA, C_KA, C_VA, C_CQ, C_CKV, C_GA, C_GB, C_KR, N_IN = 0, 512, 768, 1024, 1280, 1536, 2560, 3584, 3712

VMEM_LIMIT = 56 * 1024 * 1024

TM_IN = 512
TM_MIX = 512
TQ_B = 512
TK_B = 512
TL_ROUTE = 128
TM_PEER = 512
TE_PEER = 2048
SUB_PEER = 512


def _dot(a, b):
    return jnp.dot(a, b, preferred_element_type=F32)


def _dot_nt(a, b):
    return lax.dot_general(a, b, (((1,), (1,)), ((), ())), preferred_element_type=F32)


def _params(sem):
    return pltpu.CompilerParams(dimension_semantics=sem, vmem_limit_bytes=VMEM_LIMIT)


def _ada_kernel(c_ref, w_ref, b_ref, o_ref):
    c = c_ref[...]
    s = c * jax.nn.sigmoid(c)
    w = w_ref[0]
    s_hi = s.astype(BF16)
    s_lo = (s - s_hi.astype(F32)).astype(BF16)
    w_hi = w.astype(BF16)
    w_lo = (w - w_hi.astype(F32)).astype(BF16)
    o_ref[0] = _dot(s_hi, w_hi) + _dot(s_lo, w_hi) + _dot(s_hi, w_lo) + b_ref[0]


def _ada(c_all, ada_w, ada_b):
    n = c_all.shape[0]
    tn = 1536
    return pl.pallas_call(
        _ada_kernel,
        grid=(DEPTH, 6 * D // tn),
        in_specs=[
            pl.BlockSpec((n, D), lambda l, j: (0, 0)),
            pl.BlockSpec((1, D, tn), lambda l, j: (l, 0, j)),
            pl.BlockSpec((1, 1, tn), lambda l, j: (l, 0, j)),
        ],
        out_specs=pl.BlockSpec((1, n, tn), lambda l, j: (l, 0, j)),
        out_shape=jax.ShapeDtypeStruct((DEPTH, n, 6 * D), F32),
        compiler_params=_params(("arbitrary", "arbitrary")),
        name="ada",
    )(c_all, ada_w, ada_b.reshape(DEPTH, 1, 6 * D))


def _rope_tile(x, cos, sin_lo, sin_hi):
    return x * cos + pltpu.roll(x, LANES - B_ROPE // 2, 1) * sin_lo + pltpu.roll(x, B_ROPE // 2, 1) * sin_hi


def _inproj_kernel(x_ref, mod_ref, g1_ref, win_ref, aqg_ref, akg_ref, onesq_ref, onesk_ref,
                   cqg_ref, ckvg_ref, wuq_ref, wuk_ref, wuv_ref, bqg_ref, bkg_ref,
                   cos_ref, sinlo_ref, sinhi_ref,
                   qa_ref, ka_ref, va_ref, qb_ref, kb_ref, vb_ref, ga_ref, gb_ref):
    x = x_ref[0]
    sh1 = mod_ref[0, :, 0:D]
    sc1 = mod_ref[0, :, D:2 * D]
    ms = jnp.mean(x * x, axis=-1, keepdims=True)
    h = (x * lax.rsqrt(ms + EPS)) * g1_ref[...]
    h = h * (1.0 + sc1) + sh1
    hb = h.astype(BF16)

    def proj(a, b):
        return _dot(hb, win_ref[:, a:b])

    z = proj(C_QA, C_KA)
    ss = _dot((z * z).astype(BF16), onesq_ref[...])
    qa_ref[0] = (z * lax.rsqrt(ss * (1.0 / A_DH) + EPS) * aqg_ref[...] * (A_DH ** -0.5)).astype(BF16)
    z = proj(C_KA, C_VA)
    ss = _dot((z * z).astype(BF16), onesk_ref[...])
    ka_ref[0] = (z * lax.rsqrt(ss * (1.0 / A_DH) + EPS) * akg_ref[...]).astype(BF16)
    va_ref[0] = proj(C_VA, C_CQ).astype(BF16)

    cos = cos_ref[...]
    sin_lo = sinlo_ref[...]
    sin_hi = sinhi_ref[...]

    z = proj(C_CQ, C_CKV)
    ms = jnp.mean(z * z, axis=-1, keepdims=True)
    cqn = (z * lax.rsqrt(ms + EPS) * cqg_ref[...]).astype(BF16)
    q = _dot(cqn, wuq_ref[...])
    heads = range(B_HEADS)
    qs = [q[:, hd * LANES:(hd + 1) * LANES] for hd in heads]
    inv = [lax.rsqrt(jnp.sum(x * x, axis=-1, keepdims=True) * (1.0 / B_QK) + EPS) for x in qs]
    bqg = bqg_ref[...] * (B_QK ** -0.5 * LOG2E)
    qs = [x * r * bqg for x, r in zip(qs, inv)]
    qs = [_rope_tile(x, cos, sin_lo, sin_hi) for x in qs]
    for hd in heads:
        qb_ref[0, :, hd * LANES:(hd + 1) * LANES] = qs[hd].astype(BF16)

    z = proj(C_CKV, C_GA)
    ms = jnp.mean(z * z, axis=-1, keepdims=True)
    ckvn = (z * lax.rsqrt(ms + EPS) * ckvg_ref[...]).astype(BF16)
    kn = _dot(ckvn, wuk_ref[...])
    vb_ref[0] = _dot(ckvn, wuv_ref[...]).astype(BF16)
    kr = proj(C_KR, N_IN)
    bkg = bkg_ref[...]
    ks = [kn[:, hd * LANES:(hd + 1) * LANES] + kr for hd in heads]
    inv = [lax.rsqrt(jnp.sum(x * x, axis=-1, keepdims=True) * (1.0 / B_QK) + EPS) for x in ks]
    ks = [x * r * bkg for x, r in zip(ks, inv)]
    ks = [_rope_tile(x, cos, sin_lo, sin_hi) for x in ks]
    for hd in heads:
        kb_ref[0, :, hd * LANES:(hd + 1) * LANES] = ks[hd].astype(BF16)

    ga_ref[0] = jax.nn.sigmoid(proj(C_GA, C_GB)).astype(BF16)
    gb_ref[0] = jax.nn.sigmoid(proj(C_GB, C_KR)).astype(BF16)


def _inproj(x, mod, lw, rope):
    B, S, _ = x.shape
    tm = TM_IN
    tok = lambda w: pl.BlockSpec((1, tm, w), lambda b, i: (b, i, 0))
    full = lambda a: pl.BlockSpec(a.shape, lambda b, i: (0,) * a.ndim)
    rope_spec = pl.BlockSpec((tm, LANES), lambda b, i: (i, 0))
    consts = [lw["norm1_g"], lw["w_in"], lw["a_qg"], lw["a_kg"], lw["ones_q"], lw["ones_k"],
              lw["cq_g"], lw["ckv_g"], lw["w_uq"], lw["w_uk"], lw["w_uv"], lw["b_qg"], lw["b_kg"]]
    widths = [512, 256, 256, 1024, 1024, 512, 1024, 1024]
    return pl.pallas_call(
        _inproj_kernel,
        grid=(B, S // tm),
        in_specs=[tok(D), pl.BlockSpec((1, 1, 6 * D), lambda b, i: (b, 0, 0))]
                 + [full(a) for a in consts] + [rope_spec] * 3,
        out_specs=[tok(w) for w in widths],
        out_shape=[jax.ShapeDtypeStruct((B, S, w), BF16) for w in widths],
        compiler_params=_params(("parallel", "parallel")),
        name="inproj",
    )(x, mod, *consts, *rope)


def _attn_a_kernel(sink_ref, q_ref, k_ref, v_ref, o_ref, bias_ref):
    n = pl.program_id(1)
    S = k_ref.shape[1]
    span = 3 * BLOCK
    group = A_HEADS // A_KV
    start = pl.multiple_of(jnp.clip((n - 1) * BLOCK, 0, S - span), BLOCK)
    kw = k_ref[0, pl.ds(start, span), :]
    vw = v_ref[0, pl.ds(start, span), :]
    rows = group * BLOCK
    lane = lax.broadcasted_iota(jnp.int32, (BLOCK, LANES), 1)
    rowhead = lax.broadcasted_iota(jnp.int32, (rows, 1), 0) // BLOCK

    @pl.when((n <= 1) | (n == pl.num_programs(1) - 1))
    def _():
        qpos = n * BLOCK + lax.broadcasted_iota(jnp.int32, (rows, span), 0) % BLOCK
        kpos = start + lax.broadcasted_iota(jnp.int32, (rows, span), 1)
        dist = jnp.abs(qpos - kpos)
        distf = dist.astype(F32)
        for j in range(A_KV):
            slope = jnp.exp2(-8.0 * (group * j + rowhead + 1).astype(F32) / A_HEADS)
            bias_ref[j] = jnp.where(dist <= WINDOW, -slope * distf, NEG_INF)

    for j in range(A_KV):
        kd = kw[:, j * LANES:(j + 1) * LANES]
        vd = vw[:, j * LANES:(j + 1) * LANES]
        parts = []
        for pp in range(group // 2):
            p = (group // 2) * j + pp
            qp = q_ref[0, :, p * LANES:(p + 1) * LANES]
            zero = jnp.zeros_like(qp)
            parts.append(jnp.where(lane < A_DH, qp, zero))
            parts.append(jnp.where(lane >= A_DH, qp, zero))
        q4 = jnp.concatenate(parts, axis=0)
        s = _dot_nt(q4, kd)
        sink = jnp.zeros((rows, 1), F32)
        for g in range(group):
            sink = jnp.where(rowhead == g, sink_ref[group * j + g], sink)
        bias = bias_ref[j]
        logits = jnp.where(bias > 0.5 * NEG_INF, s + bias, NEG_INF)
        m = jnp.maximum(jnp.max(logits, axis=-1, keepdims=True), sink)
        p = jnp.exp(logits - m)
        denom = jnp.sum(p, axis=-1, keepdims=True) + jnp.exp(sink - m)
        o = _dot((p / denom).astype(BF16), vd)
        for pp in range(group // 2):
            oe = o[(2 * pp) * BLOCK:(2 * pp + 1) * BLOCK]
            oo = o[(2 * pp + 1) * BLOCK:(2 * pp + 2) * BLOCK]
            p_idx = (group // 2) * j + pp
            o_ref[0, :, p_idx * LANES:(p_idx + 1) * LANES] = jnp.where(lane < A_DH, oe, oo).astype(BF16)


def _attn_a(qa, ka, va, sink):
    B, S, _ = qa.shape
    return pl.pallas_call(
        _attn_a_kernel,
        grid=(B, S // BLOCK),
        in_specs=[
            pl.BlockSpec(memory_space=pltpu.SMEM),
            pl.BlockSpec((1, BLOCK, 512), lambda b, n: (b, n, 0)),
            pl.BlockSpec((1, S, 256), lambda b, n: (b, 0, 0)),
            pl.BlockSpec((1, S, 256), lambda b, n: (b, 0, 0)),
        ],
        out_specs=pl.BlockSpec((1, BLOCK, 512), lambda b, n: (b, n, 0)),
        out_shape=jax.ShapeDtypeStruct((B, S, 512), BF16),
        scratch_shapes=[pltpu.VMEM((A_KV, (A_HEADS // A_KV) * BLOCK, 3 * BLOCK), F32)],
        compiler_params=_params(("arbitrary", "arbitrary")),
        name="attn_a",
    )(sink, qa, ka, va)


def _attn_b_kernel(q_ref, k_ref, v_ref, o_ref, s_ref, p_ref):
    S = k_ref.shape[1]
    tq = q_ref.shape[1]
    nk = S // TK_B
    lane = lax.broadcasted_iota(jnp.int32, (tq, LANES), 1)
    outs = []
    for hh in range(2):
        q = q_ref[0, :, hh * LANES:(hh + 1) * LANES]

        def kchunk(j, hh=hh):
            return k_ref[0, j * TK_B:(j + 1) * TK_B, hh * LANES:(hh + 1) * LANES]

        def vchunk(j):
            return v_ref[0, j * TK_B:(j + 1) * TK_B, :]

        s_ref[0] = _dot_nt(q, kchunk(0))
        m = l = alpha = acc = None
        for j in range(nk):
            slot = j % 2
            if j + 1 < nk:
                s_ref[1 - slot] = _dot_nt(q, kchunk(j + 1))
            if j >= 1:
                pv = _dot(p_ref[1 - slot], vchunk(j - 1))
                acc = pv if acc is None else acc * alpha + pv
            row_max = jnp.max(s_ref[slot], axis=-1, keepdims=True)
            m_new = row_max if m is None else jnp.maximum(m, row_max)
            p = jnp.exp2(s_ref[slot] - m_new)
            p_ref[slot] = p.astype(BF16)
            psum = p[:, 0:LANES]
            for c in range(1, TK_B // LANES):
                psum = psum + p[:, c * LANES:(c + 1) * LANES]
            alpha = None if m is None else jnp.exp2(m - m_new)
            l = psum if l is None else alpha * l + psum
            m = m_new
        pv = _dot(p_ref[(nk - 1) % 2], vchunk(nk - 1))
        acc = pv if acc is None else acc * alpha + pv
        outs.append(acc / jnp.sum(l, axis=-1, keepdims=True))
    o_ref[0] = jnp.where(lane < B_V, outs[0], outs[1]).astype(BF16)


def _attn_b(qb, kb, vb):
    B, S, _ = qb.shape
    tq = TQ_B
    return pl.pallas_call(
        _attn_b_kernel,
        grid=(B, B_HEADS // 2, S // tq),
        in_specs=[
            pl.BlockSpec((1, tq, 2 * LANES), lambda b, p, i: (b, i, p)),
            pl.BlockSpec((1, S, 2 * LANES), lambda b, p, i: (b, 0, p)),
            pl.BlockSpec((1, S, LANES), lambda b, p, i: (b, 0, p)),
        ],
        out_specs=pl.BlockSpec((1, tq, LANES), lambda b, p, i: (b, i, p)),
        out_shape=jax.ShapeDtypeStruct((B, S, B_HEADS * B_V), BF16),
        scratch_shapes=[pltpu.VMEM((2, tq, TK_B), F32), pltpu.VMEM((2, tq, TK_B), BF16)],
        compiler_params=_params(("parallel", "parallel", "arbitrary")),
        name="attn_b",
    )(qb, kb, vb)


def _mixout_kernel(x_ref, ya_ref, yb_ref, ga_ref, gb_ref, mod_ref, wpa_ref, wpb_ref, wout_ref,
                   g2_ref, wq_ref, k1_ref, k2_ref, x1_ref, h2_ref, sc_ref):
    ma = _dot(ya_ref[0], wpa_ref[...])
    mb = _dot(yb_ref[0], wpb_ref[...])
    m = ga_ref[0].astype(F32) * ma + gb_ref[0].astype(F32) * mb
    mo = _dot(m.astype(BF16), wout_ref[...])
    g1 = mod_ref[0, :, 2 * D:3 * D]
    sh2 = mod_ref[0, :, 3 * D:4 * D]
    sc2 = mod_ref[0, :, 4 * D:5 * D]
    x1 = x_ref[0] + g1 * mo
    x1_ref[0] = x1
    ms = jnp.mean(x1 * x1, axis=-1, keepdims=True)
    h2 = (x1 * lax.rsqrt(ms + EPS)) * g2_ref[...]
    h2 = (h2 * (1.0 + sc2) + sh2).astype(BF16)
    h2_ref[0] = h2
    q = _dot(h2, wq_ref[...]).astype(BF16)
    for hd in range(P_HEADS):
        for c, kref in enumerate((k1_ref, k2_ref)):
            col = (2 * hd + c) * P_HALF
            sc_ref[0, 2 * hd + c] = _dot_nt(kref[...], q[:, col:col + P_HALF])


def _mixout(x, ya, yb, ga, gb, mod, lw):
    B, S, _ = x.shape
    tm = TM_MIX
    tok = lambda w: pl.BlockSpec((1, tm, w), lambda b, i: (b, i, 0))
    full = lambda a: pl.BlockSpec(a.shape, lambda b, i: (0,) * a.ndim)
    consts = [lw["w_pa"], lw["w_pb"], lw["w_out"], lw["norm2_g"], lw["peer_wq"], lw["peer_k1"], lw["peer_k2"]]
    return pl.pallas_call(
        _mixout_kernel,
        grid=(B, S // tm),
        in_specs=[tok(D), tok(512), tok(512), tok(D), tok(D),
                  pl.BlockSpec((1, 1, 6 * D), lambda b, i: (b, 0, 0))] + [full(a) for a in consts],
        out_specs=[tok(D), tok(D), pl.BlockSpec((1, 2 * P_HEADS, N_KEYS, tm), lambda b, i: (b, 0, 0, i))],
        out_shape=[jax.ShapeDtypeStruct((B, S, D), F32), jax.ShapeDtypeStruct((B, S, D), BF16),
                   jax.ShapeDtypeStruct((B, 2 * P_HEADS, N_KEYS, S), F32)],
        compiler_params=_params(("parallel", "parallel")),
        name="mixout",
    )(x, ya, yb, ga, gb, mod, *consts)


def _extract_topk(s, key_iota):
    rank = jnp.full(s.shape, float(P_TOPK), F32)
    vals = []
    for a in range(P_TOPK):
        m = jnp.max(s, axis=0, keepdims=True)
        idx = jnp.min(jnp.where(s == m, key_iota, float(N_KEYS)), axis=0, keepdims=True)
        hit = key_iota == idx
        rank = jnp.where(hit, float(a), rank)
        s = jnp.where(hit, -jnp.inf, s)
        vals.append(m)
    return vals, rank


def _count(mask):
    return jnp.sum(jnp.where(mask, 1.0, 0.0), axis=0, keepdims=True)


def _pair_tiles(v1, v2, b8):
    v2lo = jnp.concatenate(v2[:8], axis=0)
    v2hi = jnp.concatenate(v2[8:], axis=0)
    v1hi = jnp.concatenate(v1[8:], axis=0)
    tiles = [v1[0] + v2lo, v1[0] + v2hi, v1[1] + v2lo]
    flats = [b8, b8 + 8.0, b8 + float(P_TOPK)]
    valid = [None, None, None]
    for a in range(2, 8):
        ok = b8 < float(P_TOPK // (a + 1))
        tiles.append(jnp.where(ok, v1[a] + v2lo, -jnp.inf))
        flats.append(b8 + float(a * P_TOPK))
        valid.append(ok)
    tiles.append(v1hi + v2[0])
    flats.append((b8 + 8.0) * float(P_TOPK))
    valid.append(None)
    return tiles, flats, valid


def _pair_step(tiles, flats, break_ties):
    m = tiles[0]
    for t in tiles[1:]:
        m = jnp.maximum(m, t)
    m = jnp.max(m, axis=0, keepdims=True)
    if break_ties:
        big = float(P_TOPK * P_TOPK)
        idx = None
        for t, f in zip(tiles, flats):
            cur = jnp.where(t == m, f, big)
            idx = cur if idx is None else jnp.minimum(idx, cur)
        idx = jnp.min(idx, axis=0, keepdims=True)
        tiles = [jnp.where(f == idx, -jnp.inf, t) for t, f in zip(tiles, flats)]
    else:
        tiles = [jnp.where(t == m, -jnp.inf, t) for t in tiles]
    return tiles, m


def _pair_counts(tiles, valid):
    taken = [t == -jnp.inf if ok is None else (t == -jnp.inf) & ok for t, ok in zip(tiles, valid)]
    nb = [_count(taken[0]) + _count(taken[1]), _count(taken[2])] + [_count(taken[a + 1]) for a in range(2, 8)]
    last = jnp.where(taken[9], 1.0, 0.0)
    nb += [last[a:a + 1] for a in range(8)]
    total = nb[0]
    for x in nb[1:]:
        total = total + x
    return nb, total


def _key_step(s):
    m = jnp.max(s, axis=0, keepdims=True)
    return jnp.where(s == m, -jnp.inf, s), m


def _route_fast(sc_ref, e1_ref, nb_ref, e2_ref, r2_ref, b8):
    tl = sc_ref.shape[3]
    want = float(P_TOPK)
    bad = jnp.zeros((1, tl), jnp.bool_)
    s1, s2, v1, v2 = sc_ref[0, 0], sc_ref[0, 1], [], []
    for a in range(P_TOPK):
        s1, m1 = _key_step(s1)
        s2, m2 = _key_step(s2)
        v1.append(m1)
        v2.append(m2)
    for hd in range(P_HEADS):
        tiles, flats, valid = _pair_tiles(v1, v2, b8)
        more = hd + 1 < P_HEADS
        if more:
            t1, t2, w1, w2 = sc_ref[0, 2 * hd + 2], sc_ref[0, 2 * hd + 3], [], []
        m0 = None
        z = jnp.zeros((1, tl), F32)
        for k in range(P_TOPK):
            tiles, m = _pair_step(tiles, flats, False)
            if m0 is None:
                m0 = m
            z = z + jnp.exp(m - m0)
            if more:
                t1, m1 = _key_step(t1)
                t2, m2 = _key_step(t2)
                w1.append(m1)
                w2.append(m2)
        nb, total = _pair_counts(tiles, valid)
        bad = bad | (_count(s1 == -jnp.inf) != want) | (_count(s2 == -jnp.inf) != want) | (total != want)
        o1 = sc_ref[0, 2 * hd]
        o2 = sc_ref[0, 2 * hd + 1]
        nb1 = jnp.zeros((N_KEYS, tl), F32)
        rank2 = jnp.zeros((N_KEYS, tl), F32)
        for a in range(P_TOPK):
            nb1 = jnp.where(o1 == v1[a], nb[a], nb1)
            rank2 = jnp.where(v2[a] > o2, float(a + 1), rank2)
        e1_ref[0, hd] = jnp.exp(o1 - v1[0]) / (z + z)
        nb_ref[0, hd] = nb1
        e2_ref[0, hd] = jnp.exp(o2 - v2[0]).astype(BF16)
        r2_ref[0, hd] = rank2.astype(BF16)
        if more:
            s1, s2, v1, v2 = t1, t2, w1, w2
    return bad


def _route_head(s1, s2, key_iota, b8):
    tl = s1.shape[1]
    v1, rank1 = _extract_topk(s1, key_iota)
    v2, rank2 = _extract_topk(s2, key_iota)
    tiles, flats, valid = _pair_tiles(v1, v2, b8)
    m0 = None
    z = jnp.zeros((1, tl), F32)
    for k in range(P_TOPK):
        tiles, m = _pair_step(tiles, flats, True)
        if m0 is None:
            m0 = m
        z = z + jnp.exp(m - m0)
    nb, _ = _pair_counts(tiles, valid)
    nb1 = jnp.zeros((N_KEYS, tl), F32)
    for a in range(P_TOPK):
        nb1 = jnp.where(rank1 == float(a), nb[a], nb1)
    e1 = jnp.exp(s1 - v1[0]) / (z + z)
    e2 = jnp.exp(s2 - v2[0])
    return e1, nb1, e2, rank2


def _route_kernel(sc_ref, e1_ref, nb_ref, e2_ref, r2_ref):
    tl = sc_ref.shape[3]
    key_iota = lax.broadcasted_iota(jnp.int32, (N_KEYS, tl), 0).astype(F32)
    b8 = lax.broadcasted_iota(jnp.int32, (8, tl), 0).astype(F32)

    bad = _route_fast(sc_ref, e1_ref, nb_ref, e2_ref, r2_ref, b8)

    @pl.when(jnp.max(jnp.where(bad, 1.0, 0.0)) > 0.0)
    def _():
        def head(hd, carry):
            e1, nb1, e2, rank2 = _route_head(sc_ref[0, 2 * hd], sc_ref[0, 2 * hd + 1], key_iota, b8)
            e1_ref[0, hd] = e1
            nb_ref[0, hd] = nb1
            e2_ref[0, hd] = e2.astype(BF16)
            r2_ref[0, hd] = rank2.astype(BF16)
            return carry

        lax.fori_loop(0, P_HEADS, head, 0)


def _route(scores):
    B, _, _, S = scores.shape
    tl = TL_ROUTE
    spec = pl.BlockSpec((1, P_HEADS, N_KEYS, tl), lambda b, i: (b, 0, 0, i))
    shp = lambda dt: jax.ShapeDtypeStruct((B, P_HEADS, N_KEYS, S), dt)
    return pl.pallas_call(
        _route_kernel,
        grid=(B, S // tl),
        in_specs=[pl.BlockSpec((1, 2 * P_HEADS, N_KEYS, tl), lambda b, i: (b, 0, 0, i))],
        out_specs=[spec] * 4,
        out_shape=[shp(F32), shp(F32), shp(BF16), shp(BF16)],
        compiler_params=_params(("parallel", "parallel")),
        name="route",
    )(scores)


def _peer_kernel(h_ref, e1_ref, nb_ref, e2_ref, r2_ref, u_ref, vt_ref, x1_ref, mod_ref, y_ref,
                 acc_ref, bc_ref, mt_ref, a_ref):
    e = pl.program_id(2)
    rows_sub = SUB_PEER // N_KEYS
    tm = h_ref.shape[1]

    @pl.when(e == 0)
    def _():
        acc_ref[...] = jnp.zeros_like(acc_ref)

    h = h_ref[0]
    zero = jnp.zeros((BF16_ROWS, tm), BF16)
    nsub = TE_PEER // SUB_PEER

    def scores(k):
        return _dot_nt(u_ref[k * SUB_PEER:(k + 1) * SUB_PEER, :], h)

    a_ref[0] = scores(0)
    for k in range(nsub):
        if k >= 1:
            acc_ref[...] += _dot(vt_ref[:, (k - 1) * SUB_PEER:k * SUB_PEER], mt_ref[k - 1])
        if k + 1 < nsub:
            a_ref[(k + 1) % 2] = scores(k + 1)

        for hd in range(P_HEADS):
            for r in range(rows_sub):
                idx = 2 * (hd * rows_sub + r)
                row = k * rows_sub + r
                bc_ref[k, idx] = jnp.broadcast_to(nb_ref[0, hd, row:row + 1, :], (BF16_ROWS, tm)).astype(BF16)
                bc_ref[k, idx + 1] = jnp.broadcast_to(e1_ref[0, hd, row:row + 1, :], (BF16_ROWS, tm)).astype(BF16)

        for ib in range(N_KEYS // BF16_ROWS):
            lo = ib * BF16_ROWS
            w = [None] * rows_sub
            for hd in range(P_HEADS):
                r2 = r2_ref[0, hd, lo:lo + BF16_ROWS, :]
                e2 = e2_ref[0, hd, lo:lo + BF16_ROWS, :]
                for r in range(rows_sub):
                    idx = 2 * (hd * rows_sub + r)
                    t = jnp.where(r2 < bc_ref[k, idx], e2, zero) * bc_ref[k, idx + 1]
                    w[r] = t if w[r] is None else w[r] + t
            for r in range(rows_sub):
                ar = a_ref[k % 2, r * N_KEYS + lo:r * N_KEYS + lo + BF16_ROWS, :]
                g = ar * (1.0 + lax.erf(ar * (1.0 / math.sqrt(2.0))))
                mt_ref[k, r * N_KEYS + lo:r * N_KEYS + lo + BF16_ROWS, :] = w[r] * g.astype(BF16)
    acc_ref[...] += _dot(vt_ref[:, (nsub - 1) * SUB_PEER:nsub * SUB_PEER], mt_ref[nsub - 1])

    @pl.when(e == pl.num_programs(2) - 1)
    def _():
        g2 = mod_ref[0, :, 5 * D:6 * D]
        y_ref[0] = x1_ref[0] + g2 * acc_ref[...].T


def _peer(h2, e1, nb, e2, r2, x1, mod, lw):
    B, S, _ = h2.shape
    tm, te = TM_PEER, TE_PEER
    rows = te // N_KEYS
    tok = pl.BlockSpec((1, tm, D), lambda b, i, e: (b, i, 0))
    slab = pl.BlockSpec((1, P_HEADS, rows, tm), lambda b, i, e: (b, 0, e, i))
    dense = pl.BlockSpec((1, P_HEADS, N_KEYS, tm), lambda b, i, e: (b, 0, 0, i))
    return pl.pallas_call(
        _peer_kernel,
        grid=(B, S // tm, N_EXPERTS // te),
        in_specs=[tok, slab, slab, dense, dense,
                  pl.BlockSpec((te, D), lambda b, i, e: (e, 0)),
                  pl.BlockSpec((D, te), lambda b, i, e: (0, e)),
                  tok, pl.BlockSpec((1, 1, 6 * D), lambda b, i, e: (b, 0, 0))],
        out_specs=tok,
        out_shape=jax.ShapeDtypeStruct((B, S, D), F32),
        scratch_shapes=[pltpu.VMEM((D, tm), F32),
                        pltpu.VMEM((te // SUB_PEER, 2 * P_HEADS * (SUB_PEER // N_KEYS), BF16_ROWS, tm), BF16),
                        pltpu.VMEM((te // SUB_PEER, SUB_PEER, tm), BF16),
                        pltpu.VMEM((2, SUB_PEER, tm), F32)],
        compiler_params=_params(("parallel", "parallel", "arbitrary")),
        name="peer",
    )(h2, e1, nb, e2, r2, lw["peer_u"], lw["peer_vt"], x1, mod)


def _block_ones(n, blk):
    i = jnp.arange(n) // blk
    return (i[:, None] == i[None, :]).astype(BF16)


def _pad_cols(w, n):
    return jnp.concatenate([w, jnp.zeros((w.shape[0], n - w.shape[1]), w.dtype)], axis=1)


def _layer_weights(l, norm1_g, norm2_g, w_in, a_q_norm, a_k_norm, cq_norm, ckv_norm, w_uq, w_ukv,
                   b_q_norm, b_k_norm, w_pa, w_pb, w_out, peer_wq, peer_k1, peer_k2, peer_u, peer_v):
    w = w_in[l]
    k0, k1 = w[:, 512:576], w[:, 576:640]
    v0, v1 = w[:, 640:704], w[:, 704:768]
    z64 = jnp.zeros((D, B_NOPE), F32)
    z32 = jnp.zeros((D, LANES - B_QK), F32)
    w_in_r = jnp.concatenate([w[:, 0:512], k0, k0, k1, k1, v0, v0, v1, v1, w[:, 768:1024], w[:, 1024:1280],
                              w[:, 1312:2336], w[:, 2336:3360], z64, w[:, 1280:1312], z32], axis=1).astype(BF16)
    uq = w_uq[l].reshape(Q_LORA, B_HEADS, B_QK)
    w_uq_r = jnp.pad(uq, ((0, 0), (0, 0), (0, LANES - B_QK))).reshape(Q_LORA, B_HEADS * LANES).astype(BF16)
    ukv = w_ukv[l].reshape(KV_LORA, B_HEADS, B_NOPE + B_V)
    w_uk_r = jnp.pad(ukv[:, :, :B_NOPE], ((0, 0), (0, 0), (0, LANES - B_NOPE))).reshape(KV_LORA, B_HEADS * LANES).astype(BF16)
    w_uv_r = ukv[:, :, B_NOPE:].reshape(KV_LORA, B_HEADS * B_V).astype(BF16)
    row = lambda a: a.reshape(1, -1)
    return {
        "norm1_g": row(norm1_g[l]), "norm2_g": row(norm2_g[l]), "w_in": w_in_r,
        "a_qg": row(jnp.tile(a_q_norm[l], A_HEADS)), "a_kg": row(jnp.tile(a_k_norm[l], 2 * A_KV)),
        "ones_q": _block_ones(A_HEADS * A_DH, A_DH), "ones_k": _block_ones(2 * A_KV * A_DH, A_DH),
        "cq_g": row(cq_norm[l]), "ckv_g": row(ckv_norm[l]),
        "w_uq": w_uq_r, "w_uk": w_uk_r, "w_uv": w_uv_r,
        "b_qg": row(_pad_cols(b_q_norm[l][None], LANES)), "b_kg": row(_pad_cols(b_k_norm[l][None], LANES)),
        "w_pa": w_pa[l].astype(BF16), "w_pb": w_pb[l].astype(BF16), "w_out": w_out[l].astype(BF16),
        "peer_wq": peer_wq[l].astype(BF16), "peer_k1": peer_k1[l].astype(BF16), "peer_k2": peer_k2[l].astype(BF16),
        "peer_u": peer_u[l].astype(BF16), "peer_vt": peer_v[l].astype(BF16).T,
    }


def _rope_tables(S):
    half = B_ROPE // 2
    inv = jnp.power(jnp.float32(ROPE_BASE), -jnp.arange(half, dtype=F32) / half)
    ang = jnp.arange(S, dtype=F32)[:, None] * inv[None, :]
    cos, sin = jnp.cos(ang), jnp.sin(ang)
    zeros = jnp.zeros((S, half), F32)
    pad = jnp.zeros((S, LANES - B_QK), F32)
    cos_t = jnp.concatenate([jnp.ones((S, B_NOPE), F32), cos, cos, pad + 1.0], axis=1)
    sin_lo = jnp.concatenate([jnp.zeros((S, B_NOPE), F32), -sin, zeros, pad], axis=1)
    sin_hi = jnp.concatenate([jnp.zeros((S, B_NOPE), F32), zeros, sin, pad], axis=1)
    return cos_t, sin_lo, sin_hi


def _trunk(x, mods, layers, sinks):
    rope = _rope_tables(x.shape[1])
    for l in range(DEPTH):
        lw = layers[l]
        mod = mods[l]
        qa, ka, va, qb, kb, vb, ga, gb = _inproj(x, mod, lw, rope)
        ya = _attn_a(qa, ka, va, sinks[l])
        yb = _attn_b(qb, kb, vb)
        x1, h2, scores = _mixout(x, ya, yb, ga, gb, mod, lw)
        e1, nb, e2, r2 = _route(scores)
        x = _peer(h2, e1, nb, e2, r2, x1, mod, lw)
    return x


def kernel(x_prompt, x_sample, c_prompt, c_sample, ada_w, ada_b, norm1_g, norm2_g, w_in, a_q_norm, a_k_norm, a_sink, cq_norm, ckv_norm, w_uq, w_ukv, b_q_norm, b_k_norm, w_pa, w_pb, w_out, peer_wq, peer_k1, peer_k2, peer_u, peer_v):
    nb_prompt = c_prompt.shape[0]
    mod_all = _ada(jnp.concatenate([c_prompt, c_sample], axis=0), ada_w, ada_b)
    mods_p = [mod_all[l, :nb_prompt].reshape(nb_prompt, 1, 6 * D) for l in range(DEPTH)]
    mods_s = [mod_all[l, nb_prompt:].reshape(-1, 1, 6 * D) for l in range(DEPTH)]
    layers = [_layer_weights(l, norm1_g, norm2_g, w_in, a_q_norm, a_k_norm, cq_norm, ckv_norm, w_uq, w_ukv,
                             b_q_norm, b_k_norm, w_pa, w_pb, w_out, peer_wq, peer_k1, peer_k2, peer_u, peer_v)
              for l in range(DEPTH)]
    sinks = [a_sink[l] for l in range(DEPTH)]
    y_prompt = _trunk(x_prompt, mods_p, layers, sinks)
    y_sample = _trunk(x_sample, mods_s, layers, sinks)
    return (y_prompt, y_sample)
```

```python
import functools
import math

import jax
import jax.numpy as jnp
from jax import lax
from jax.experimental import pallas as pl
from jax.experimental.pallas import tpu as pltpu

F32 = jnp.float32
BF16 = jnp.bfloat16

D = 1024
DEPTH = 2
EPS = 1e-6
NEG_INF = -1e30
LOG2E = 1.4426950408889634
LANES = 128
BF16_ROWS = 16
BLOCK = 128
A_HEADS, A_KV, A_DH = 8, 2, 64
WINDOW = 128
B_HEADS, B_NOPE, B_ROPE, B_V = 8, 64, 32, 64
B_QK = B_NOPE + B_ROPE
Q_LORA = KV_LORA = 256
ROPE_BASE = 10000.0
P_HEADS, P_HALF, N_KEYS, P_TOPK = 8, 128, 128, 16
N_EXPERTS = N_KEYS * N_KEYS

C_QA, C_KA, C_VA, C_CQ, C_CKV, C_GA, C_GB, C_KR, N_IN = 0, 512, 768, 1024, 1280, 1536, 2560, 3584, 3712

VMEM_LIMIT = 56 * 1024 * 1024

TM_IN = 512
TM_MIX = 512
TQ_B = 512
TK_B = 512
TL_ROUTE = 128
TM_PEER = 512
TE_PEER = 2048
SUB_PEER = 512


def _dot(a, b):
    return jnp.dot(a, b, preferred_element_type=F32)


def _dot_nt(a, b):
    return lax.dot_general(a, b, (((1,), (1,)), ((), ())), preferred_element_type=F32)


def _params(sem):
    return pltpu.CompilerParams(dimension_semantics=sem, vmem_limit_bytes=VMEM_LIMIT)


def _ada_kernel(c_ref, w_ref, b_ref, o_ref):
    c = c_ref[...]
    s = c * jax.nn.sigmoid(c)
    w = w_ref[0]
    s_hi = s.astype(BF16)
    s_lo = (s - s_hi.astype(F32)).astype(BF16)
    w_hi = w.astype(BF16)
    w_lo = (w - w_hi.astype(F32)).astype(BF16)
    o_ref[0] = _dot(s_hi, w_hi) + _dot(s_lo, w_hi) + _dot(s_hi, w_lo) + b_ref[0]


def _ada(c_all, ada_w, ada_b):
    n = c_all.shape[0]
    tn = 1536
    return pl.pallas_call(
        _ada_kernel,
        grid=(DEPTH, 6 * D // tn),
        in_specs=[
            pl.BlockSpec((n, D), lambda l, j: (0, 0)),
            pl.BlockSpec((1, D, tn), lambda l, j: (l, 0, j)),
            pl.BlockSpec((1, 1, tn), lambda l, j: (l, 0, j)),
        ],
        out_specs=pl.BlockSpec((1, n, tn), lambda l, j: (l, 0, j)),
        out_shape=jax.ShapeDtypeStruct((DEPTH, n, 6 * D), F32),
        compiler_params=_params(("arbitrary", "arbitrary")),
        name="ada",
    )(c_all, ada_w, ada_b.reshape(DEPTH, 1, 6 * D))


def _rope_tile(x, cos, sin_lo, sin_hi):
    return x * cos + pltpu.roll(x, LANES - B_ROPE // 2, 1) * sin_lo + pltpu.roll(x, B_ROPE // 2, 1) * sin_hi


def _inproj_kernel(x_ref, mod_ref, g1_ref, win_ref, aqg_ref, akg_ref, onesq_ref, onesk_ref,
                   cqg_ref, ckvg_ref, wuq_ref, wuk_ref, wuv_ref, bqg_ref, bkg_ref,
                   cos_ref, sinlo_ref, sinhi_ref,
                   qa_ref, ka_ref, va_ref, qb_ref, kb_ref, vb_ref, ga_ref, gb_ref):
    x = x_ref[0]
    sh1 = mod_ref[0, :, 0:D]
    sc1 = mod_ref[0, :, D:2 * D]
    ms = jnp.mean(x * x, axis=-1, keepdims=True)
    h = (x * lax.rsqrt(ms + EPS)) * g1_ref[...]
    h = h * (1.0 + sc1) + sh1
    hb = h.astype(BF16)

    def proj(a, b):
        return _dot(hb, win_ref[:, a:b])

    z = proj(C_QA, C_KA)
    ss = _dot((z * z).astype(BF16), onesq_ref[...])
    qa_ref[0] = (z * lax.rsqrt(ss * (1.0 / A_DH) + EPS) * aqg_ref[...] * (A_DH ** -0.5)).astype(BF16)
    z = proj(C_KA, C_VA)
    ss = _dot((z * z).astype(BF16), onesk_ref[...])
    ka_ref[0] = (z * lax.rsqrt(ss * (1.0 / A_DH) + EPS) * akg_ref[...]).astype(BF16)
    va_ref[0] = proj(C_VA, C_CQ).astype(BF16)

    cos = cos_ref[...]
    sin_lo = sinlo_ref[...]
    sin_hi = sinhi_ref[...]

    z = proj(C_CQ, C_CKV)
    ms = jnp.mean(z * z, axis=-1, keepdims=True)
    cqn = (z * lax.rsqrt(ms + EPS) * cqg_ref[...]).astype(BF16)
    q = _dot(cqn, wuq_ref[...])
    heads = range(B_HEADS)
    qs = [q[:, hd * LANES:(hd + 1) * LANES] for hd in heads]
    inv = [lax.rsqrt(jnp.sum(x * x, axis=-1, keepdims=True) * (1.0 / B_QK) + EPS) for x in qs]
    bqg = bqg_ref[...] * (B_QK ** -0.5 * LOG2E)
    qs = [x * r * bqg for x, r in zip(qs, inv)]
    qs = [_rope_tile(x, cos, sin_lo, sin_hi) for x in qs]
    for hd in heads:
        qb_ref[0, :, hd * LANES:(hd + 1) * LANES] = qs[hd].astype(BF16)

    z = proj(C_CKV, C_GA)
    ms = jnp.mean(z * z, axis=-1, keepdims=True)
    ckvn = (z * lax.rsqrt(ms + EPS) * ckvg_ref[...]).astype(BF16)
    kn = _dot(ckvn, wuk_ref[...])
    vb_ref[0] = _dot(ckvn, wuv_ref[...]).astype(BF16)
    kr = proj(C_KR, N_IN)
    bkg = bkg_ref[...]
    ks = [kn[:, hd * LANES:(hd + 1) * LANES] + kr for hd in heads]
    inv = [lax.rsqrt(jnp.sum(x * x, axis=-1, keepdims=True) * (1.0 / B_QK) + EPS) for x in ks]
    ks = [x * r * bkg for x, r in zip(ks, inv)]
    ks = [_rope_tile(x, cos, sin_lo, sin_hi) for x in ks]
    for hd in heads:
        kb_ref[0, :, hd * LANES:(hd + 1) * LANES] = ks[hd].astype(BF16)

    ga_ref[0] = jax.nn.sigmoid(proj(C_GA, C_GB)).astype(BF16)
    gb_ref[0] = jax.nn.sigmoid(proj(C_GB, C_KR)).astype(BF16)


def _inproj(x, mod, lw, rope):
    B, S, _ = x.shape
    tm = TM_IN
    tok = lambda w: pl.BlockSpec((1, tm, w), lambda b, i: (b, i, 0))
    full = lambda a: pl.BlockSpec(a.shape, lambda b, i: (0,) * a.ndim)
    rope_spec = pl.BlockSpec((tm, LANES), lambda b, i: (i, 0))
    consts = [lw["norm1_g"], lw["w_in"], lw["a_qg"], lw["a_kg"], lw["ones_q"], lw["ones_k"],
              lw["cq_g"], lw["ckv_g"], lw["w_uq"], lw["w_uk"], lw["w_uv"], lw["b_qg"], lw["b_kg"]]
    widths = [512, 256, 256, 1024, 1024, 512, 1024, 1024]
    return pl.pallas_call(
        _inproj_kernel,
        grid=(B, S // tm),
        in_specs=[tok(D), pl.BlockSpec((1, 1, 6 * D), lambda b, i: (b, 0, 0))]
                 + [full(a) for a in consts] + [rope_spec] * 3,
        out_specs=[tok(w) for w in widths],
        out_shape=[jax.ShapeDtypeStruct((B, S, w), BF16) for w in widths],
        compiler_params=_params(("parallel", "parallel")),
        name="inproj",
    )(x, mod, *consts, *rope)


def _attn_a_kernel(sink_ref, q_ref, k_ref, v_ref, o_ref, bias_ref):
    n = pl.program_id(1)
    S = k_ref.shape[1]
    span = 3 * BLOCK
    group = A_HEADS // A_KV
    start = pl.multiple_of(jnp.clip((n - 1) * BLOCK, 0, S - span), BLOCK)
    kw = k_ref[0, pl.ds(start, span), :]
    vw = v_ref[0, pl.ds(start, span), :]
    rows = group * BLOCK
    lane = lax.broadcasted_iota(jnp.int32, (BLOCK, LANES), 1)
    rowhead = lax.broadcasted_iota(jnp.int32, (rows, 1), 0) // BLOCK

    @pl.when((n <= 1) | (n == pl.num_programs(1) - 1))
    def _():
        qpos = n * BLOCK + lax.broadcasted_iota(jnp.int32, (rows, span), 0) % BLOCK
        kpos = start + lax.broadcasted_iota(jnp.int32, (rows, span), 1)
        dist = jnp.abs(qpos - kpos)
        distf = dist.astype(F32)
        for j in range(A_KV):
            slope = jnp.exp2(-8.0 * (group * j + rowhead + 1).astype(F32) / A_HEADS)
            bias_ref[j] = jnp.where(dist <= WINDOW, -slope * distf, NEG_INF)

    scores = []
    for j in range(A_KV):
        kd = kw[:, j * LANES:(j + 1) * LANES]
        parts = []
        for pp in range(group // 2):
            p = (group // 2) * j + pp
            qp = q_ref[0, :, p * LANES:(p + 1) * LANES]
            zero = jnp.zeros_like(qp)
            parts.append(jnp.where(lane < A_DH, qp, zero))
            parts.append(jnp.where(lane >= A_DH, qp, zero))
        scores.append(_dot_nt(jnp.concatenate(parts, axis=0), kd))

    for j in range(A_KV):
        vd = vw[:, j * LANES:(j + 1) * LANES]
        sink = jnp.zeros((rows, 1), F32)
        for g in range(group):
            sink = jnp.where(rowhead == g, sink_ref[group * j + g], sink)
        bias = bias_ref[j]
        logits = jnp.where(bias > 0.5 * NEG_INF, scores[j] + bias, NEG_INF)
        m = jnp.maximum(jnp.max(logits, axis=-1, keepdims=True), sink)
        p = jnp.exp(logits - m)
        denom = jnp.sum(p, axis=-1, keepdims=True) + jnp.exp(sink - m)
        o = _dot(p.astype(BF16), vd) / denom
        for pp in range(group // 2):
            oe = o[(2 * pp) * BLOCK:(2 * pp + 1) * BLOCK]
            oo = o[(2 * pp + 1) * BLOCK:(2 * pp + 2) * BLOCK]
            p_idx = (group // 2) * j + pp
            o_ref[0, :, p_idx * LANES:(p_idx + 1) * LANES] = jnp.where(lane < A_DH, oe, oo).astype(BF16)


def _attn_a(qa, ka, va, sink):
    B, S, _ = qa.shape
    return pl.pallas_call(
        _attn_a_kernel,
        grid=(B, S // BLOCK),
        in_specs=[
            pl.BlockSpec(memory_space=pltpu.SMEM),
            pl.BlockSpec((1, BLOCK, 512), lambda b, n: (b, n, 0)),
            pl.BlockSpec((1, S, 256), lambda b, n: (b, 0, 0)),
            pl.BlockSpec((1, S, 256), lambda b, n: (b, 0, 0)),
        ],
        out_specs=pl.BlockSpec((1, BLOCK, 512), lambda b, n: (b, n, 0)),
        out_shape=jax.ShapeDtypeStruct((B, S, 512), BF16),
        scratch_shapes=[pltpu.VMEM((A_KV, (A_HEADS // A_KV) * BLOCK, 3 * BLOCK), F32)],
        compiler_params=_params(("arbitrary", "arbitrary")),
        name="attn_a",
    )(sink, qa, ka, va)


def _attn_b_kernel(q_ref, k_ref, v_ref, o_ref, s_ref, p_ref):
    S = k_ref.shape[1]
    tq = q_ref.shape[1]
    nk = S // TK_B
    lane = lax.broadcasted_iota(jnp.int32, (tq, LANES), 1)
    outs = []
    for hh in range(2):
        q = q_ref[0, :, hh * LANES:(hh + 1) * LANES]

        def kchunk(j, hh=hh):
            return k_ref[0, j * TK_B:(j + 1) * TK_B, hh * LANES:(hh + 1) * LANES]

        def vchunk(j):
            return v_ref[0, j * TK_B:(j + 1) * TK_B, :]

        s_ref[0] = _dot_nt(q, kchunk(0))
        m = l = alpha = acc = None
        for j in range(nk):
            slot = j % 2
            if j + 1 < nk:
                s_ref[1 - slot] = _dot_nt(q, kchunk(j + 1))
            if j >= 1:
                pv = _dot(p_ref[1 - slot], vchunk(j - 1))
                acc = pv if acc is None else acc * alpha + pv
            row_max = jnp.max(s_ref[slot], axis=-1, keepdims=True)
            m_new = row_max if m is None else jnp.maximum(m, row_max)
            p = jnp.exp2(s_ref[slot] - m_new)
            p_ref[slot] = p.astype(BF16)
            psum = p[:, 0:LANES]
            for c in range(1, TK_B // LANES):
                psum = psum + p[:, c * LANES:(c + 1) * LANES]
            alpha = None if m is None else jnp.exp2(m - m_new)
            l = psum if l is None else alpha * l + psum
            m = m_new
        pv = _dot(p_ref[(nk - 1) % 2], vchunk(nk - 1))
        acc = pv if acc is None else acc * alpha + pv
        outs.append(acc / jnp.sum(l, axis=-1, keepdims=True))
    o_ref[0] = jnp.where(lane < B_V, outs[0], outs[1]).astype(BF16)


def _attn_b(qb, kb, vb):
    B, S, _ = qb.shape
    tq = TQ_B
    return pl.pallas_call(
        _attn_b_kernel,
        grid=(B, B_HEADS // 2, S // tq),
        in_specs=[
            pl.BlockSpec((1, tq, 2 * LANES), lambda b, p, i: (b, i, p)),
            pl.BlockSpec((1, S, 2 * LANES), lambda b, p, i: (b, 0, p)),
            pl.BlockSpec((1, S, LANES), lambda b, p, i: (b, 0, p)),
        ],
        out_specs=pl.BlockSpec((1, tq, LANES), lambda b, p, i: (b, i, p)),
        out_shape=jax.ShapeDtypeStruct((B, S, B_HEADS * B_V), BF16),
        scratch_shapes=[pltpu.VMEM((2, tq, TK_B), F32), pltpu.VMEM((2, tq, TK_B), BF16)],
        compiler_params=_params(("parallel", "parallel", "arbitrary")),
        name="attn_b",
    )(qb, kb, vb)


def _mixout_kernel(x_ref, ya_ref, yb_ref, ga_ref, gb_ref, mod_ref, wpa_ref, wpb_ref, wout_ref,
                   g2_ref, wq_ref, k1_ref, k2_ref, x1_ref, h2_ref, sc_ref):
    ma = _dot(ya_ref[0], wpa_ref[...])
    mb = _dot(yb_ref[0], wpb_ref[...])
    m = ga_ref[0].astype(F32) * ma + gb_ref[0].astype(F32) * mb
    mo = _dot(m.astype(BF16), wout_ref[...])
    g1 = mod_ref[0, :, 2 * D:3 * D]
    sh2 = mod_ref[0, :, 3 * D:4 * D]
    sc2 = mod_ref[0, :, 4 * D:5 * D]
    x1 = x_ref[0] + g1 * mo
    x1_ref[0] = x1
    ms = jnp.mean(x1 * x1, axis=-1, keepdims=True)
    h2 = (x1 * lax.rsqrt(ms + EPS)) * g2_ref[...]
    h2 = (h2 * (1.0 + sc2) + sh2).astype(BF16)
    h2_ref[0] = h2
    q = _dot(h2, wq_ref[...]).astype(BF16)
    for hd in range(P_HEADS):
        for c, kref in enumerate((k1_ref, k2_ref)):
            col = (2 * hd + c) * P_HALF
            sc_ref[0, 2 * hd + c] = _dot_nt(kref[...], q[:, col:col + P_HALF])


def _mixout(x, ya, yb, ga, gb, mod, lw):
    B, S, _ = x.shape
    tm = TM_MIX
    tok = lambda w: pl.BlockSpec((1, tm, w), lambda b, i: (b, i, 0))
    full = lambda a: pl.BlockSpec(a.shape, lambda b, i: (0,) * a.ndim)
    consts = [lw["w_pa"], lw["w_pb"], lw["w_out"], lw["norm2_g"], lw["peer_wq"], lw["peer_k1"], lw["peer_k2"]]
    return pl.pallas_call(
        _mixout_kernel,
        grid=(B, S // tm),
        in_specs=[tok(D), tok(512), tok(512), tok(D), tok(D),
                  pl.BlockSpec((1, 1, 6 * D), lambda b, i: (b, 0, 0))] + [full(a) for a in consts],
        out_specs=[tok(D), tok(D), pl.BlockSpec((1, 2 * P_HEADS, N_KEYS, tm), lambda b, i: (b, 0, 0, i))],
        out_shape=[jax.ShapeDtypeStruct((B, S, D), F32), jax.ShapeDtypeStruct((B, S, D), BF16),
                   jax.ShapeDtypeStruct((B, 2 * P_HEADS, N_KEYS, S), F32)],
        compiler_params=_params(("parallel", "parallel")),
        name="mixout",
    )(x, ya, yb, ga, gb, mod, *consts)


def _extract_topk(s, key_iota):
    rank = jnp.full(s.shape, float(P_TOPK), F32)
    vals = []
    for a in range(P_TOPK):
        m = jnp.max(s, axis=0, keepdims=True)
        idx = jnp.min(jnp.where(s == m, key_iota, float(N_KEYS)), axis=0, keepdims=True)
        hit = key_iota == idx
        rank = jnp.where(hit, float(a), rank)
        s = jnp.where(hit, -jnp.inf, s)
        vals.append(m)
    return vals, rank


def _count(mask):
    return jnp.sum(jnp.where(mask, 1.0, 0.0), axis=0, keepdims=True)


def _pair_tiles(v1, v2, b8):
    v2lo = jnp.concatenate(v2[:8], axis=0)
    v2hi = jnp.concatenate(v2[8:], axis=0)
    v1hi = jnp.concatenate(v1[8:], axis=0)
    tiles = [v1[0] + v2lo, v1[0] + v2hi, v1[1] + v2lo]
    flats = [b8, b8 + 8.0, b8 + float(P_TOPK)]
    valid = [None, None, None]
    for a in range(2, 8):
        ok = b8 < float(P_TOPK // (a + 1))
        tiles.append(jnp.where(ok, v1[a] + v2lo, -jnp.inf))
        flats.append(b8 + float(a * P_TOPK))
        valid.append(ok)
    tiles.append(v1hi + v2[0])
    flats.append((b8 + 8.0) * float(P_TOPK))
    valid.append(None)
    return tiles, flats, valid


def _pair_step(tiles, flats, break_ties):
    m = tiles[0]
    for t in tiles[1:]:
        m = jnp.maximum(m, t)
    m = jnp.max(m, axis=0, keepdims=True)
    if break_ties:
        big = float(P_TOPK * P_TOPK)
        idx = None
        for t, f in zip(tiles, flats):
            cur = jnp.where(t == m, f, big)
            idx = cur if idx is None else jnp.minimum(idx, cur)
        idx = jnp.min(idx, axis=0, keepdims=True)
        tiles = [jnp.where(f == idx, -jnp.inf, t) for t, f in zip(tiles, flats)]
    else:
        tiles = [jnp.where(t == m, -jnp.inf, t) for t in tiles]
    return tiles, m


def _pair_counts(tiles, valid):
    taken = [t == -jnp.inf if ok is None else (t == -jnp.inf) & ok for t, ok in zip(tiles, valid)]
    nb = [_count(taken[0]) + _count(taken[1]), _count(taken[2])] + [_count(taken[a + 1]) for a in range(2, 8)]
    last = jnp.where(taken[9], 1.0, 0.0)
    nb += [last[a:a + 1] for a in range(8)]
    total = nb[0]
    for x in nb[1:]:
        total = total + x
    return nb, total


def _key_step(s):
    m = jnp.max(s, axis=0, keepdims=True)
    return jnp.where(s == m, -jnp.inf, s), m


def _route_fast(sc_ref, e1_ref, nb_ref, e2_ref, r2_ref, b8):
    tl = sc_ref.shape[3]
    want = float(P_TOPK)
    bad = jnp.zeros((1, tl), jnp.bool_)
    s1, s2, v1, v2 = sc_ref[0, 0], sc_ref[0, 1], [], []
    for a in range(P_TOPK):
        s1, m1 = _key_step(s1)
        s2, m2 = _key_step(s2)
        v1.append(m1)
        v2.append(m2)
    for hd in range(P_HEADS):
        tiles, flats, valid = _pair_tiles(v1, v2, b8)
        more = hd + 1 < P_HEADS
        if more:
            t1, t2, w1, w2 = sc_ref[0, 2 * hd + 2], sc_ref[0, 2 * hd + 3], [], []
        m0 = None
        z = jnp.zeros((1, tl), F32)
        for k in range(P_TOPK):
            tiles, m = _pair_step(tiles, flats, False)
            if m0 is None:
                m0 = m
            z = z + jnp.exp(m - m0)
            if more:
                t1, m1 = _key_step(t1)
                t2, m2 = _key_step(t2)
                w1.append(m1)
                w2.append(m2)
        nb, total = _pair_counts(tiles, valid)
        bad = bad | (_count(s1 == -jnp.inf) != want) | (_count(s2 == -jnp.inf) != want) | (total != want)
        o1 = sc_ref[0, 2 * hd]
        o2 = sc_ref[0, 2 * hd + 1]
        nb1 = jnp.zeros((N_KEYS, tl), F32)
        rank2 = jnp.zeros((N_KEYS, tl), F32)
        for a in range(P_TOPK):
            nb1 = jnp.where(o1 == v1[a], nb[a], nb1)
            rank2 = jnp.where(v2[a] > o2, float(a + 1), rank2)
        e1_ref[0, hd] = jnp.exp(o1 - v1[0]) / (z + z)
        nb_ref[0, hd] = nb1
        e2_ref[0, hd] = jnp.exp(o2 - v2[0]).astype(BF16)
        r2_ref[0, hd] = rank2.astype(BF16)
        if more:
            s1, s2, v1, v2 = t1, t2, w1, w2
    return bad


def _route_head(s1, s2, key_iota, b8):
    tl = s1.shape[1]
    v1, rank1 = _extract_topk(s1, key_iota)
    v2, rank2 = _extract_topk(s2, key_iota)
    tiles, flats, valid = _pair_tiles(v1, v2, b8)
    m0 = None
    z = jnp.zeros((1, tl), F32)
    for k in range(P_TOPK):
        tiles, m = _pair_step(tiles, flats, True)
        if m0 is None:
            m0 = m
        z = z + jnp.exp(m - m0)
    nb, _ = _pair_counts(tiles, valid)
    nb1 = jnp.zeros((N_KEYS, tl), F32)
    for a in range(P_TOPK):
        nb1 = jnp.where(rank1 == float(a), nb[a], nb1)
    e1 = jnp.exp(s1 - v1[0]) / (z + z)
    e2 = jnp.exp(s2 - v2[0])
    return e1, nb1, e2, rank2


def _route_kernel(sc_ref, e1_ref, nb_ref, e2_ref, r2_ref):
    tl = sc_ref.shape[3]
    key_iota = lax.broadcasted_iota(jnp.int32, (N_KEYS, tl), 0).astype(F32)
    b8 = lax.broadcasted_iota(jnp.int32, (8, tl), 0).astype(F32)

    bad = _route_fast(sc_ref, e1_ref, nb_ref, e2_ref, r2_ref, b8)

    @pl.when(jnp.max(jnp.where(bad, 1.0, 0.0)) > 0.0)
    def _():
        def head(hd, carry):
            e1, nb1, e2, rank2 = _route_head(sc_ref[0, 2 * hd], sc_ref[0, 2 * hd + 1], key_iota, b8)
            e1_ref[0, hd] = e1
            nb_ref[0, hd] = nb1
            e2_ref[0, hd] = e2.astype(BF16)
            r2_ref[0, hd] = rank2.astype(BF16)
            return carry

        lax.fori_loop(0, P_HEADS, head, 0)


def _route(scores):
    B, _, _, S = scores.shape
    tl = TL_ROUTE
    spec = pl.BlockSpec((1, P_HEADS, N_KEYS, tl), lambda b, i: (b, 0, 0, i))
    shp = lambda dt: jax.ShapeDtypeStruct((B, P_HEADS, N_KEYS, S), dt)
    return pl.pallas_call(
        _route_kernel,
        grid=(B, S // tl),
        in_specs=[pl.BlockSpec((1, 2 * P_HEADS, N_KEYS, tl), lambda b, i: (b, 0, 0, i))],
        out_specs=[spec] * 4,
        out_shape=[shp(F32), shp(F32), shp(BF16), shp(BF16)],
        compiler_params=_params(("parallel", "parallel")),
        name="route",
    )(scores)


def _peer_kernel(h_ref, e1_ref, nb_ref, e2_ref, r2_ref, u_ref, vt_ref, x1_ref, mod_ref, y_ref,
                 acc_ref, bc_ref, mt_ref, a_ref):
    e = pl.program_id(2)
    rows_sub = SUB_PEER // N_KEYS
    tm = h_ref.shape[1]

    @pl.when(e == 0)
    def _():
        acc_ref[...] = jnp.zeros_like(acc_ref)

    h = h_ref[0]
    zero = jnp.zeros((BF16_ROWS, tm), BF16)
    nsub = TE_PEER // SUB_PEER

    def scores(k):
        return _dot_nt(u_ref[k * SUB_PEER:(k + 1) * SUB_PEER, :], h)

    a_ref[0] = scores(0)
    for k in range(nsub):
        if k >= 1:
            acc_ref[...] += _dot(vt_ref[:, (k - 1) * SUB_PEER:k * SUB_PEER], mt_ref[k - 1])
        if k + 1 < nsub:
            a_ref[(k + 1) % 2] = scores(k + 1)

        for hd in range(P_HEADS):
            for r in range(rows_sub):
                idx = 2 * (hd * rows_sub + r)
                row = k * rows_sub + r
                bc_ref[k, idx] = jnp.broadcast_to(nb_ref[0, hd, row:row + 1, :], (BF16_ROWS, tm)).astype(BF16)
                bc_ref[k, idx + 1] = jnp.broadcast_to(e1_ref[0, hd, row:row + 1, :], (BF16_ROWS, tm)).astype(BF16)

        for ib in range(N_KEYS // BF16_ROWS):
            lo = ib * BF16_ROWS
            w = [None] * rows_sub
            for hd in range(P_HEADS):
                r2 = r2_ref[0, hd, lo:lo + BF16_ROWS, :]
                e2 = e2_ref[0, hd, lo:lo + BF16_ROWS, :]
                for r in range(rows_sub):
                    idx = 2 * (hd * rows_sub + r)
                    t = jnp.where(r2 < bc_ref[k, idx], e2, zero) * bc_ref[k, idx + 1]
                    w[r] = t if w[r] is None else w[r] + t
            for r in range(rows_sub):
                ar = a_ref[k % 2, r * N_KEYS + lo:r * N_KEYS + lo + BF16_ROWS, :]
                g = ar * (1.0 + lax.erf(ar * (1.0 / math.sqrt(2.0))))
                mt_ref[k, r * N_KEYS + lo:r * N_KEYS + lo + BF16_ROWS, :] = w[r] * g.astype(BF16)
    acc_ref[...] += _dot(vt_ref[:, (nsub - 1) * SUB_PEER:nsub * SUB_PEER], mt_ref[nsub - 1])

    @pl.when(e == pl.num_programs(2) - 1)
    def _():
        g2 = mod_ref[0, :, 5 * D:6 * D]
        y_ref[0] = x1_ref[0] + g2 * acc_ref[...].T


def _peer(h2, e1, nb, e2, r2, x1, mod, lw):
    B, S, _ = h2.shape
    tm, te = TM_PEER, TE_PEER
    rows = te // N_KEYS
    tok = pl.BlockSpec((1, tm, D), lambda b, i, e: (b, i, 0))
    slab = pl.BlockSpec((1, P_HEADS, rows, tm), lambda b, i, e: (b, 0, e, i))
    dense = pl.BlockSpec((1, P_HEADS, N_KEYS, tm), lambda b, i, e: (b, 0, 0, i))
    return pl.pallas_call(
        _peer_kernel,
        grid=(B, S // tm, N_EXPERTS // te),
        in_specs=[tok, slab, slab, dense, dense,
                  pl.BlockSpec((te, D), lambda b, i, e: (e, 0)),
                  pl.BlockSpec((D, te), lambda b, i, e: (0, e)),
                  tok, pl.BlockSpec((1, 1, 6 * D), lambda b, i, e: (b, 0, 0))],
        out_specs=tok,
        out_shape=jax.ShapeDtypeStruct((B, S, D), F32),
        scratch_shapes=[pltpu.VMEM((D, tm), F32),
                        pltpu.VMEM((te // SUB_PEER, 2 * P_HEADS * (SUB_PEER // N_KEYS), BF16_ROWS, tm), BF16),
                        pltpu.VMEM((te // SUB_PEER, SUB_PEER, tm), BF16),
                        pltpu.VMEM((2, SUB_PEER, tm), F32)],
        compiler_params=_params(("parallel", "parallel", "arbitrary")),
        name="peer",
    )(h2, e1, nb, e2, r2, lw["peer_u"], lw["peer_vt"], x1, mod)


def _block_ones(n, blk):
    i = jnp.arange(n) // blk
    return (i[:, None] == i[None, :]).astype(BF16)


def _pad_cols(w, n):
    return jnp.concatenate([w, jnp.zeros((w.shape[0], n - w.shape[1]), w.dtype)], axis=1)


def _layer_weights(l, norm1_g, norm2_g, w_in, a_q_norm, a_k_norm, cq_norm, ckv_norm, w_uq, w_ukv,
                   b_q_norm, b_k_norm, w_pa, w_pb, w_out, peer_wq, peer_k1, peer_k2, peer_u, peer_v):
    w = w_in[l]
    k0, k1 = w[:, 512:576], w[:, 576:640]
    v0, v1 = w[:, 640:704], w[:, 704:768]
    z64 = jnp.zeros((D, B_NOPE), F32)
    z32 = jnp.zeros((D, LANES - B_QK), F32)
    w_in_r = jnp.concatenate([w[:, 0:512], k0, k0, k1, k1, v0, v0, v1, v1, w[:, 768:1024], w[:, 1024:1280],
                              w[:, 1312:2336], w[:, 2336:3360], z64, w[:, 1280:1312], z32], axis=1).astype(BF16)
    uq = w_uq[l].reshape(Q_LORA, B_HEADS, B_QK)
    w_uq_r = jnp.pad(uq, ((0, 0), (0, 0), (0, LANES - B_QK))).reshape(Q_LORA, B_HEADS * LANES).astype(BF16)
    ukv = w_ukv[l].reshape(KV_LORA, B_HEADS, B_NOPE + B_V)
    w_uk_r = jnp.pad(ukv[:, :, :B_NOPE], ((0, 0), (0, 0), (0, LANES - B_NOPE))).reshape(KV_LORA, B_HEADS * LANES).astype(BF16)
    w_uv_r = ukv[:, :, B_NOPE:].reshape(KV_LORA, B_HEADS * B_V).astype(BF16)
    row = lambda a: a.reshape(1, -1)
    return {
        "norm1_g": row(norm1_g[l]), "norm2_g": row(norm2_g[l]), "w_in": w_in_r,
        "a_qg": row(jnp.tile(a_q_norm[l], A_HEADS)), "a_kg": row(jnp.tile(a_k_norm[l], 2 * A_KV)),
        "ones_q": _block_ones(A_HEADS * A_DH, A_DH), "ones_k": _block_ones(2 * A_KV * A_DH, A_DH),
        "cq_g": row(cq_norm[l]), "ckv_g": row(ckv_norm[l]),
        "w_uq": w_uq_r, "w_uk": w_uk_r, "w_uv": w_uv_r,
        "b_qg": row(_pad_cols(b_q_norm[l][None], LANES)), "b_kg": row(_pad_cols(b_k_norm[l][None], LANES)),
        "w_pa": w_pa[l].astype(BF16), "w_pb": w_pb[l].astype(BF16), "w_out": w_out[l].astype(BF16),
        "peer_wq": peer_wq[l].astype(BF16), "peer_k1": peer_k1[l].astype(BF16), "peer_k2": peer_k2[l].astype(BF16),
        "peer_u": peer_u[l].astype(BF16), "peer_vt": peer_v[l].astype(BF16).T,
    }


def _rope_tables(S):
    half = B_ROPE // 2
    inv = jnp.power(jnp.float32(ROPE_BASE), -jnp.arange(half, dtype=F32) / half)
    ang = jnp.arange(S, dtype=F32)[:, None] * inv[None, :]
    cos, sin = jnp.cos(ang), jnp.sin(ang)
    zeros = jnp.zeros((S, half), F32)
    pad = jnp.zeros((S, LANES - B_QK), F32)
    cos_t = jnp.concatenate([jnp.ones((S, B_NOPE), F32), cos, cos, pad + 1.0], axis=1)
    sin_lo = jnp.concatenate([jnp.zeros((S, B_NOPE), F32), -sin, zeros, pad], axis=1)
    sin_hi = jnp.concatenate([jnp.zeros((S, B_NOPE), F32), zeros, sin, pad], axis=1)
    return cos_t, sin_lo, sin_hi


def _trunk(x, mods, layers, sinks):
    rope = _rope_tables(x.shape[1])
    for l in range(DEPTH):
        lw = layers[l]
        mod = mods[l]
        qa, ka, va, qb, kb, vb, ga, gb = _inproj(x, mod, lw, rope)
        ya = _attn_a(qa, ka, va, sinks[l])
        yb = _attn_b(qb, kb, vb)
        x1, h2, scores = _mixout(x, ya, yb, ga, gb, mod, lw)
        e1, nb, e2, r2 = _route(scores)
        x = _peer(h2, e1, nb, e2, r2, x1, mod, lw)
    return x


def kernel(x_prompt, x_sample, c_prompt, c_sample, ada_w, ada_b, norm1_g, norm2_g, w_in, a_q_norm, a_k_norm, a_sink, cq_norm, ckv_norm, w_uq, w_ukv, b_q_norm, b_k_norm, w_pa, w_pb, w_out, peer_wq, peer_k1, peer_k2, peer_u, peer_v):
    nb_prompt = c_prompt.shape[0]
    mod_all = _ada(jnp.concatenate([c_prompt, c_sample], axis=0), ada_w, ada_b)
    mods_p = [mod_all[l, :nb_prompt].reshape(nb_prompt, 1, 6 * D) for l in range(DEPTH)]
    mods_s = [mod_all[l, nb_prompt:].reshape(-1, 1, 6 * D) for l in range(DEPTH)]
    layers = [_layer_weights(l, norm1_g, norm2_g, w_in, a_q_norm, a_k_norm, cq_norm, ckv_norm, w_uq, w_ukv,
                             b_q_norm, b_k_norm, w_pa, w_pb, w_out, peer_wq, peer_k1, peer_k2, peer_u, peer_v)
              for l in range(DEPTH)]
    sinks = [a_sink[l] for l in range(DEPTH)]
    y_prompt = _trunk(x_prompt, mods_p, layers, sinks)
    y_sample = _trunk(x_sample, mods_s, layers, sinks)
    return (y_prompt, y_sample)
```

```python
import functools
import math

import jax
import jax.numpy as jnp
from jax import lax
from jax.experimental import pallas as pl
from jax.experimental.pallas import tpu as pltpu

F32 = jnp.float32
BF16 = jnp.bfloat16

D = 1024
DEPTH = 2
EPS = 1e-6
NEG_INF = -1e30
LOG2E = 1.4426950408889634
LANES = 128
BF16_ROWS = 16
BLOCK = 128
A_HEADS, A_KV, A_DH = 8, 2, 64
WINDOW = 128
B_HEADS, B_NOPE, B_ROPE, B_V = 8, 64, 32, 64
B_QK = B_NOPE + B_ROPE
Q_LORA = KV_LORA = 256
ROPE_BASE = 10000.0
P_HEADS, P_HALF, N_KEYS, P_TOPK = 8, 128, 128, 16
N_EXPERTS = N_KEYS * N_KEYS

C_QA, C_KA, C_VA, C_CQ, C_CKV, C_GA, C_GB, C_KR, N_IN = 0, 512, 768, 1024, 1280, 1536, 2560, 3584, 3712

VMEM_LIMIT = 56 * 1024 * 1024

TM_IN = 512
TM_MIX = 512
TQ_B = 512
TK_B = 512
TL_ROUTE = 128
TM_PEER = 512
TE_PEER = 2048
SUB_PEER = 512


def _dot(a, b):
    return jnp.dot(a, b, preferred_element_type=F32)


def _dot_nt(a, b):
    return lax.dot_general(a, b, (((1,), (1,)), ((), ())), preferred_element_type=F32)


def _params(sem):
    return pltpu.CompilerParams(dimension_semantics=sem, vmem_limit_bytes=VMEM_LIMIT)


def _ada_kernel(c_ref, w_ref, b_ref, o_ref):
    c = c_ref[...]
    s = c * jax.nn.sigmoid(c)
    w = w_ref[0]
    s_hi = s.astype(BF16)
    s_lo = (s - s_hi.astype(F32)).astype(BF16)
    w_hi = w.astype(BF16)
    w_lo = (w - w_hi.astype(F32)).astype(BF16)
    o_ref[0] = _dot(s_hi, w_hi) + _dot(s_lo, w_hi) + _dot(s_hi, w_lo) + b_ref[0]


def _ada(c_all, ada_w, ada_b):
    n = c_all.shape[0]
    tn = 1536
    return pl.pallas_call(
        _ada_kernel,
        grid=(DEPTH, 6 * D // tn),
        in_specs=[
            pl.BlockSpec((n, D), lambda l, j: (0, 0)),
            pl.BlockSpec((1, D, tn), lambda l, j: (l, 0, j)),
            pl.BlockSpec((1, 1, tn), lambda l, j: (l, 0, j)),
        ],
        out_specs=pl.BlockSpec((1, n, tn), lambda l, j: (l, 0, j)),
        out_shape=jax.ShapeDtypeStruct((DEPTH, n, 6 * D), F32),
        compiler_params=_params(("arbitrary", "arbitrary")),
        name="ada",
    )(c_all, ada_w, ada_b.reshape(DEPTH, 1, 6 * D))


def _rope_tile(x, cos, sin_lo, sin_hi):
    return x * cos + pltpu.roll(x, LANES - B_ROPE // 2, 1) * sin_lo + pltpu.roll(x, B_ROPE // 2, 1) * sin_hi


def _inproj_kernel(x_ref, mod_ref, g1_ref, win_ref, aqg_ref, akg_ref, onesq_ref, onesk_ref,
                   cqg_ref, ckvg_ref, wuq_ref, wuk_ref, wuv_ref, bqg_ref, bkg_ref,
                   cos_ref, sinlo_ref, sinhi_ref,
                   qa_ref, ka_ref, va_ref, qb_ref, kb_ref, vb_ref, ga_ref, gb_ref):
    x = x_ref[0]
    sh1 = mod_ref[0, :, 0:D]
    sc1 = mod_ref[0, :, D:2 * D]
    ms = jnp.mean(x * x, axis=-1, keepdims=True)
    h = (x * lax.rsqrt(ms + EPS)) * g1_ref[...]
    h = h * (1.0 + sc1) + sh1
    hb = h.astype(BF16)

    def proj(a, b):
        return _dot(hb, win_ref[:, a:b])

    z = proj(C_QA, C_KA)
    ss = _dot((z * z).astype(BF16), onesq_ref[...])
    qa_ref[0] = (z * lax.rsqrt(ss * (1.0 / A_DH) + EPS) * aqg_ref[...] * (A_DH ** -0.5)).astype(BF16)
    z = proj(C_KA, C_VA)
    ss = _dot((z * z).astype(BF16), onesk_ref[...])
    ka_ref[0] = (z * lax.rsqrt(ss * (1.0 / A_DH) + EPS) * akg_ref[...]).astype(BF16)
    va_ref[0] = proj(C_VA, C_CQ).astype(BF16)

    cos = cos_ref[...]
    sin_lo = sinlo_ref[...]
    sin_hi = sinhi_ref[...]

    z = proj(C_CQ, C_CKV)
    ms = jnp.mean(z * z, axis=-1, keepdims=True)
    cqn = (z * lax.rsqrt(ms + EPS) * cqg_ref[...]).astype(BF16)
    q = _dot(cqn, wuq_ref[...])
    heads = range(B_HEADS)
    qs = [q[:, hd * LANES:(hd + 1) * LANES] for hd in heads]
    inv = [lax.rsqrt(jnp.sum(x * x, axis=-1, keepdims=True) * (1.0 / B_QK) + EPS) for x in qs]
    bqg = bqg_ref[...] * (B_QK ** -0.5 * LOG2E)
    qs = [x * r * bqg for x, r in zip(qs, inv)]
    qs = [_rope_tile(x, cos, sin_lo, sin_hi) for x in qs]
    for hd in heads:
        qb_ref[0, :, hd * LANES:(hd + 1) * LANES] = qs[hd].astype(BF16)

    z = proj(C_CKV, C_GA)
    ms = jnp.mean(z * z, axis=-1, keepdims=True)
    ckvn = (z * lax.rsqrt(ms + EPS) * ckvg_ref[...]).astype(BF16)
    kn = _dot(ckvn, wuk_ref[...])
    vb_ref[0] = _dot(ckvn, wuv_ref[...]).astype(BF16)
    kr = proj(C_KR, N_IN)
    bkg = bkg_ref[...]
    ks = [kn[:, hd * LANES:(hd + 1) * LANES] + kr for hd in heads]
    inv = [lax.rsqrt(jnp.sum(x * x, axis=-1, keepdims=True) * (1.0 / B_QK) + EPS) for x in ks]
    ks = [x * r * bkg for x, r in zip(ks, inv)]
    ks = [_rope_tile(x, cos, sin_lo, sin_hi) for x in ks]
    for hd in heads:
        kb_ref[0, :, hd * LANES:(hd + 1) * LANES] = ks[hd].astype(BF16)

    ga_ref[0] = jax.nn.sigmoid(proj(C_GA, C_GB)).astype(BF16)
    gb_ref[0] = jax.nn.sigmoid(proj(C_GB, C_KR)).astype(BF16)


def _inproj(x, mod, lw, rope):
    B, S, _ = x.shape
    tm = TM_IN
    tok = lambda w: pl.BlockSpec((1, tm, w), lambda b, i: (b, i, 0))
    full = lambda a: pl.BlockSpec(a.shape, lambda b, i: (0,) * a.ndim)
    rope_spec = pl.BlockSpec((tm, LANES), lambda b, i: (i, 0))
    consts = [lw["norm1_g"], lw["w_in"], lw["a_qg"], lw["a_kg"], lw["ones_q"], lw["ones_k"],
              lw["cq_g"], lw["ckv_g"], lw["w_uq"], lw["w_uk"], lw["w_uv"], lw["b_qg"], lw["b_kg"]]
    widths = [512, 256, 256, 1024, 1024, 512, 1024, 1024]
    return pl.pallas_call(
        _inproj_kernel,
        grid=(B, S // tm),
        in_specs=[tok(D), pl.BlockSpec((1, 1, 6 * D), lambda b, i: (b, 0, 0))]
                 + [full(a) for a in consts] + [rope_spec] * 3,
        out_specs=[tok(w) for w in widths],
        out_shape=[jax.ShapeDtypeStruct((B, S, w), BF16) for w in widths],
        compiler_params=_params(("parallel", "parallel")),
        name="inproj",
    )(x, mod, *consts, *rope)


def _attn_a_kernel(sink_ref, q_ref, k_ref, v_ref, o_ref, bias_ref):
    n = pl.program_id(1)
    S = k_ref.shape[1]
    span = 3 * BLOCK
    group = A_HEADS // A_KV
    start = pl.multiple_of(jnp.clip((n - 1) * BLOCK, 0, S - span), BLOCK)
    kw = k_ref[0, pl.ds(start, span), :]
    vw = v_ref[0, pl.ds(start, span), :]
    rows = group * BLOCK
    lane = lax.broadcasted_iota(jnp.int32, (BLOCK, LANES), 1)
    rowhead = lax.broadcasted_iota(jnp.int32, (rows, 1), 0) // BLOCK

    @pl.when((n <= 1) | (n == pl.num_programs(1) - 1))
    def _():
        qpos = n * BLOCK + lax.broadcasted_iota(jnp.int32, (rows, span), 0) % BLOCK
        kpos = start + lax.broadcasted_iota(jnp.int32, (rows, span), 1)
        dist = jnp.abs(qpos - kpos)
        distf = dist.astype(F32)
        for j in range(A_KV):
            slope = jnp.exp2(-8.0 * (group * j + rowhead + 1).astype(F32) / A_HEADS)
            bias_ref[j] = jnp.where(dist <= WINDOW, -slope * distf, NEG_INF)

    scores = []
    for j in range(A_KV):
        kd = kw[:, j * LANES:(j + 1) * LANES]
        parts = []
        for pp in range(group // 2):
            p = (group // 2) * j + pp
            qp = q_ref[0, :, p * LANES:(p + 1) * LANES]
            zero = jnp.zeros_like(qp)
            parts.append(jnp.where(lane < A_DH, qp, zero))
            parts.append(jnp.where(lane >= A_DH, qp, zero))
        scores.append(_dot_nt(jnp.concatenate(parts, axis=0), kd))

    for j in range(A_KV):
        vd = vw[:, j * LANES:(j + 1) * LANES]
        sink = jnp.zeros((rows, 1), F32)
        for g in range(group):
            sink = jnp.where(rowhead == g, sink_ref[group * j + g], sink)
        bias = bias_ref[j]
        logits = jnp.where(bias > 0.5 * NEG_INF, scores[j] + bias, NEG_INF)
        m = jnp.maximum(jnp.max(logits, axis=-1, keepdims=True), sink)
        p = jnp.exp(logits - m)
        denom = jnp.sum(p, axis=-1, keepdims=True) + jnp.exp(sink - m)
        o = _dot(p.astype(BF16), vd) / denom
        for pp in range(group // 2):
            oe = o[(2 * pp) * BLOCK:(2 * pp + 1) * BLOCK]
            oo = o[(2 * pp + 1) * BLOCK:(2 * pp + 2) * BLOCK]
            p_idx = (group // 2) * j + pp
            o_ref[0, :, p_idx * LANES:(p_idx + 1) * LANES] = jnp.where(lane < A_DH, oe, oo).astype(BF16)


def _attn_a(qa, ka, va, sink):
    B, S, _ = qa.shape
    return pl.pallas_call(
        _attn_a_kernel,
        grid=(B, S // BLOCK),
        in_specs=[
            pl.BlockSpec(memory_space=pltpu.SMEM),
            pl.BlockSpec((1, BLOCK, 512), lambda b, n: (b, n, 0)),
            pl.BlockSpec((1, S, 256), lambda b, n: (b, 0, 0)),
            pl.BlockSpec((1, S, 256), lambda b, n: (b, 0, 0)),
        ],
        out_specs=pl.BlockSpec((1, BLOCK, 512), lambda b, n: (b, n, 0)),
        out_shape=jax.ShapeDtypeStruct((B, S, 512), BF16),
        scratch_shapes=[pltpu.VMEM((A_KV, (A_HEADS // A_KV) * BLOCK, 3 * BLOCK), F32)],
        compiler_params=_params(("arbitrary", "arbitrary")),
        name="attn_a",
    )(sink, qa, ka, va)


def _attn_b_kernel(q_ref, k_ref, v_ref, o_ref, s_ref, p_ref):
    S = k_ref.shape[1]
    tq = q_ref.shape[1]
    nk = S // TK_B
    lane = lax.broadcasted_iota(jnp.int32, (tq, LANES), 1)
    outs = []
    for hh in range(2):
        q = q_ref[0, :, hh * LANES:(hh + 1) * LANES]

        def kchunk(j, hh=hh):
            return k_ref[0, j * TK_B:(j + 1) * TK_B, hh * LANES:(hh + 1) * LANES]

        def vchunk(j):
            return v_ref[0, j * TK_B:(j + 1) * TK_B, :]

        s_ref[0] = _dot_nt(q, kchunk(0))
        m = l = alpha = acc = None
        for j in range(nk):
            slot = j % 2
            if j + 1 < nk:
                s_ref[1 - slot] = _dot_nt(q, kchunk(j + 1))
            if j >= 1:
                pv = _dot(p_ref[1 - slot], vchunk(j - 1))
                acc = pv if acc is None else acc * alpha + pv
            row_max = jnp.max(s_ref[slot], axis=-1, keepdims=True)
            m_new = row_max if m is None else jnp.maximum(m, row_max)
            p = jnp.exp2(s_ref[slot] - m_new)
            p_ref[slot] = p.astype(BF16)
            psum = p[:, 0:LANES]
            for c in range(1, TK_B // LANES):
                psum = psum + p[:, c * LANES:(c + 1) * LANES]
            alpha = None if m is None else jnp.exp2(m - m_new)
            l = psum if l is None else alpha * l + psum
            m = m_new
        pv = _dot(p_ref[(nk - 1) % 2], vchunk(nk - 1))
        acc = pv if acc is None else acc * alpha + pv
        outs.append(acc / jnp.sum(l, axis=-1, keepdims=True))
    o_ref[0] = jnp.where(lane < B_V, outs[0], outs[1]).astype(BF16)


def _attn_b(qb, kb, vb):
    B, S, _ = qb.shape
    tq = TQ_B
    return pl.pallas_call(
        _attn_b_kernel,
        grid=(B, B_HEADS // 2, S // tq),
        in_specs=[
            pl.BlockSpec((1, tq, 2 * LANES), lambda b, p, i: (b, i, p)),
            pl.BlockSpec((1, S, 2 * LANES), lambda b, p, i: (b, 0, p)),
            pl.BlockSpec((1, S, LANES), lambda b, p, i: (b, 0, p)),
        ],
        out_specs=pl.BlockSpec((1, tq, LANES), lambda b, p, i: (b, i, p)),
        out_shape=jax.ShapeDtypeStruct((B, S, B_HEADS * B_V), BF16),
        scratch_shapes=[pltpu.VMEM((2, tq, TK_B), F32), pltpu.VMEM((2, tq, TK_B), BF16)],
        compiler_params=_params(("parallel", "parallel", "arbitrary")),
        name="attn_b",
    )(qb, kb, vb)


def _mixout_kernel(x_ref, ya_ref, yb_ref, ga_ref, gb_ref, mod_ref, wpa_ref, wpb_ref, wout_ref,
                   g2_ref, wq_ref, k1_ref, k2_ref, x1_ref, h2_ref, sc_ref):
    ma = _dot(ya_ref[0], wpa_ref[...])
    mb = _dot(yb_ref[0], wpb_ref[...])
    m = ga_ref[0].astype(F32) * ma + gb_ref[0].astype(F32) * mb
    mo = _dot(m.astype(BF16), wout_ref[...])
    g1 = mod_ref[0, :, 2 * D:3 * D]
    sh2 = mod_ref[0, :, 3 * D:4 * D]
    sc2 = mod_ref[0, :, 4 * D:5 * D]
    x1 = x_ref[0] + g1 * mo
    x1_ref[0] = x1
    ms = jnp.mean(x1 * x1, axis=-1, keepdims=True)
    h2 = (x1 * lax.rsqrt(ms + EPS)) * g2_ref[...]
    h2 = (h2 * (1.0 + sc2) + sh2).astype(BF16)
    h2_ref[0] = h2
    q = _dot(h2, wq_ref[...]).astype(BF16)
    for hd in range(P_HEADS):
        for c, kref in enumerate((k1_ref, k2_ref)):
            col = (2 * hd + c) * P_HALF
            sc_ref[0, 2 * hd + c] = _dot_nt(kref[...], q[:, col:col + P_HALF])


def _mixout(x, ya, yb, ga, gb, mod, lw):
    B, S, _ = x.shape
    tm = TM_MIX
    tok = lambda w: pl.BlockSpec((1, tm, w), lambda b, i: (b, i, 0))
    full = lambda a: pl.BlockSpec(a.shape, lambda b, i: (0,) * a.ndim)
    consts = [lw["w_pa"], lw["w_pb"], lw["w_out"], lw["norm2_g"], lw["peer_wq"], lw["peer_k1"], lw["peer_k2"]]
    return pl.pallas_call(
        _mixout_kernel,
        grid=(B, S // tm),
        in_specs=[tok(D), tok(512), tok(512), tok(D), tok(D),
                  pl.BlockSpec((1, 1, 6 * D), lambda b, i: (b, 0, 0))] + [full(a) for a in consts],
        out_specs=[tok(D), tok(D), pl.BlockSpec((1, 2 * P_HEADS, N_KEYS, tm), lambda b, i: (b, 0, 0, i))],
        out_shape=[jax.ShapeDtypeStruct((B, S, D), F32), jax.ShapeDtypeStruct((B, S, D), BF16),
                   jax.ShapeDtypeStruct((B, 2 * P_HEADS, N_KEYS, S), F32)],
        compiler_params=_params(("parallel", "parallel")),
        name="mixout",
    )(x, ya, yb, ga, gb, mod, *consts)


def _extract_topk(s, key_iota):
    rank = jnp.full(s.shape, float(P_TOPK), F32)
    vals = []
    for a in range(P_TOPK):
        m = jnp.max(s, axis=0, keepdims=True)
        idx = jnp.min(jnp.where(s == m, key_iota, float(N_KEYS)), axis=0, keepdims=True)
        hit = key_iota == idx
        rank = jnp.where(hit, float(a), rank)
        s = jnp.where(hit, -jnp.inf, s)
        vals.append(m)
    return vals, rank


def _count(mask):
    return jnp.sum(jnp.where(mask, 1.0, 0.0), axis=0, keepdims=True)


def _pair_tiles(v1, v2, b8):
    v2lo = jnp.concatenate(v2[:8], axis=0)
    v2hi = jnp.concatenate(v2[8:], axis=0)
    v1hi = jnp.concatenate(v1[8:], axis=0)
    tiles = [v1[0] + v2lo, v1[0] + v2hi, v1[1] + v2lo]
    flats = [b8, b8 + 8.0, b8 + float(P_TOPK)]
    valid = [None, None, None]
    for a in range(2, 8):
        ok = b8 < float(P_TOPK // (a + 1))
        tiles.append(jnp.where(ok, v1[a] + v2lo, -jnp.inf))
        flats.append(b8 + float(a * P_TOPK))
        valid.append(ok)
    tiles.append(v1hi + v2[0])
    flats.append((b8 + 8.0) * float(P_TOPK))
    valid.append(None)
    return tiles, flats, valid


def _pair_step(tiles, flats, break_ties):
    m = tiles[0]
    for t in tiles[1:]:
        m = jnp.maximum(m, t)
    m = jnp.max(m, axis=0, keepdims=True)
    if break_ties:
        big = float(P_TOPK * P_TOPK)
        idx = None
        for t, f in zip(tiles, flats):
            cur = jnp.where(t == m, f, big)
            idx = cur if idx is None else jnp.minimum(idx, cur)
        idx = jnp.min(idx, axis=0, keepdims=True)
        tiles = [jnp.where(f == idx, -jnp.inf, t) for t, f in zip(tiles, flats)]
    else:
        tiles = [jnp.where(t == m, -jnp.inf, t) for t in tiles]
    return tiles, m


def _pair_counts(tiles, valid):
    taken = [t == -jnp.inf if ok is None else (t == -jnp.inf) & ok for t, ok in zip(tiles, valid)]
    nb = [_count(taken[0]) + _count(taken[1]), _count(taken[2])] + [_count(taken[a + 1]) for a in range(2, 8)]
    last = jnp.where(taken[9], 1.0, 0.0)
    nb += [last[a:a + 1] for a in range(8)]
    total = nb[0]
    for x in nb[1:]:
        total = total + x
    return nb, total


def _key_next(s, m):
    return jnp.max(s if m is None else jnp.where(s < m, s, -jnp.inf), axis=0, keepdims=True)


def _route_fast(sc_ref, e1_ref, nb_ref, e2_ref, r2_ref, b8):
    tl = sc_ref.shape[3]
    want = float(P_TOPK)
    bad = jnp.zeros((1, tl), jnp.bool_)
    v1, v2, m1, m2 = [], [], None, None
    for a in range(P_TOPK):
        m1 = _key_next(sc_ref[0, 0], m1)
        m2 = _key_next(sc_ref[0, 1], m2)
        v1.append(m1)
        v2.append(m2)
    for hd in range(P_HEADS):
        tiles, flats, valid = _pair_tiles(v1, v2, b8)
        more = hd + 1 < P_HEADS
        w1, w2, m1, m2 = [], [], None, None
        m0 = None
        z = jnp.zeros((1, tl), F32)
        for k in range(P_TOPK):
            tiles, m = _pair_step(tiles, flats, False)
            if m0 is None:
                m0 = m
            z = z + jnp.exp(m - m0)
            if more:
                m1 = _key_next(sc_ref[0, 2 * hd + 2], m1)
                m2 = _key_next(sc_ref[0, 2 * hd + 3], m2)
                w1.append(m1)
                w2.append(m2)
        nb, total = _pair_counts(tiles, valid)
        o1 = sc_ref[0, 2 * hd]
        o2 = sc_ref[0, 2 * hd + 1]
        bad = bad | (_count(o1 >= v1[-1]) != want) | (_count(o2 >= v2[-1]) != want) | (total != want)
        nb1 = jnp.zeros((N_KEYS, tl), F32)
        rank2 = jnp.zeros((N_KEYS, tl), F32)
        for a in range(P_TOPK):
            nb1 = jnp.where(o1 == v1[a], nb[a], nb1)
            rank2 = jnp.where(v2[a] > o2, float(a + 1), rank2)
        e1_ref[0, hd] = jnp.exp(o1 - v1[0]) / (z + z)
        nb_ref[0, hd] = nb1
        e2_ref[0, hd] = jnp.exp(o2 - v2[0]).astype(BF16)
        r2_ref[0, hd] = rank2.astype(BF16)
        if more:
            v1, v2 = w1, w2
    return bad


def _route_head(s1, s2, key_iota, b8):
    tl = s1.shape[1]
    v1, rank1 = _extract_topk(s1, key_iota)
    v2, rank2 = _extract_topk(s2, key_iota)
    tiles, flats, valid = _pair_tiles(v1, v2, b8)
    m0 = None
    z = jnp.zeros((1, tl), F32)
    for k in range(P_TOPK):
        tiles, m = _pair_step(tiles, flats, True)
        if m0 is None:
            m0 = m
        z = z + jnp.exp(m - m0)
    nb, _ = _pair_counts(tiles, valid)
    nb1 = jnp.zeros((N_KEYS, tl), F32)
    for a in range(P_TOPK):
        nb1 = jnp.where(rank1 == float(a), nb[a], nb1)
    e1 = jnp.exp(s1 - v1[0]) / (z + z)
    e2 = jnp.exp(s2 - v2[0])
    return e1, nb1, e2, rank2


def _route_kernel(sc_ref, e1_ref, nb_ref, e2_ref, r2_ref):
    tl = sc_ref.shape[3]
    key_iota = lax.broadcasted_iota(jnp.int32, (N_KEYS, tl), 0).astype(F32)
    b8 = lax.broadcasted_iota(jnp.int32, (8, tl), 0).astype(F32)

    bad = _route_fast(sc_ref, e1_ref, nb_ref, e2_ref, r2_ref, b8)

    @pl.when(jnp.max(jnp.where(bad, 1.0, 0.0)) > 0.0)
    def _():
        def head(hd, carry):
            e1, nb1, e2, rank2 = _route_head(sc_ref[0, 2 * hd], sc_ref[0, 2 * hd + 1], key_iota, b8)
            e1_ref[0, hd] = e1
            nb_ref[0, hd] = nb1
            e2_ref[0, hd] = e2.astype(BF16)
            r2_ref[0, hd] = rank2.astype(BF16)
            return carry

        lax.fori_loop(0, P_HEADS, head, 0)


def _route(scores):
    B, _, _, S = scores.shape
    tl = TL_ROUTE
    spec = pl.BlockSpec((1, P_HEADS, N_KEYS, tl), lambda b, i: (b, 0, 0, i))
    shp = lambda dt: jax.ShapeDtypeStruct((B, P_HEADS, N_KEYS, S), dt)
    return pl.pallas_call(
        _route_kernel,
        grid=(B, S // tl),
        in_specs=[pl.BlockSpec((1, 2 * P_HEADS, N_KEYS, tl), lambda b, i: (b, 0, 0, i))],
        out_specs=[spec] * 4,
        out_shape=[shp(F32), shp(F32), shp(BF16), shp(BF16)],
        compiler_params=_params(("parallel", "parallel")),
        name="route",
    )(scores)


def _peer_kernel(h_ref, e1_ref, nb_ref, e2_ref, r2_ref, u_ref, vt_ref, x1_ref, mod_ref, y_ref,
                 acc_ref, bc_ref, mt_ref, a_ref):
    e = pl.program_id(2)
    rows_sub = SUB_PEER // N_KEYS
    tm = h_ref.shape[1]

    @pl.when(e == 0)
    def _():
        acc_ref[...] = jnp.zeros_like(acc_ref)

    h = h_ref[0]
    zero = jnp.zeros((BF16_ROWS, tm), BF16)
    nsub = TE_PEER // SUB_PEER

    def scores(k):
        return _dot_nt(u_ref[k * SUB_PEER:(k + 1) * SUB_PEER, :], h)

    a_ref[0] = scores(0)
    for k in range(nsub):
        if k >= 1:
            acc_ref[...] += _dot(vt_ref[:, (k - 1) * SUB_PEER:k * SUB_PEER], mt_ref[k - 1])
        if k + 1 < nsub:
            a_ref[(k + 1) % 2] = scores(k + 1)

        for hd in range(P_HEADS):
            for r in range(rows_sub):
                idx = 2 * (hd * rows_sub + r)
                row = k * rows_sub + r
                bc_ref[k, idx] = jnp.broadcast_to(nb_ref[0, hd, row:row + 1, :], (BF16_ROWS, tm)).astype(BF16)
                bc_ref[k, idx + 1] = jnp.broadcast_to(e1_ref[0, hd, row:row + 1, :], (BF16_ROWS, tm)).astype(BF16)

        for ib in range(N_KEYS // BF16_ROWS):
            lo = ib * BF16_ROWS
            w = [None] * rows_sub
            for hd in range(P_HEADS):
                r2 = r2_ref[0, hd, lo:lo + BF16_ROWS, :]
                e2 = e2_ref[0, hd, lo:lo + BF16_ROWS, :]
                for r in range(rows_sub):
                    idx = 2 * (hd * rows_sub + r)
                    t = jnp.where(r2 < bc_ref[k, idx], e2, zero) * bc_ref[k, idx + 1]
                    w[r] = t if w[r] is None else w[r] + t
            for r in range(rows_sub):
                ar = a_ref[k % 2, r * N_KEYS + lo:r * N_KEYS + lo + BF16_ROWS, :]
                g = ar * (1.0 + lax.erf(ar * (1.0 / math.sqrt(2.0))))
                mt_ref[k, r * N_KEYS + lo:r * N_KEYS + lo + BF16_ROWS, :] = w[r] * g.astype(BF16)
    acc_ref[...] += _dot(vt_ref[:, (nsub - 1) * SUB_PEER:nsub * SUB_PEER], mt_ref[nsub - 1])

    @pl.when(e == pl.num_programs(2) - 1)
    def _():
        g2 = mod_ref[0, :, 5 * D:6 * D]
        y_ref[0] = x1_ref[0] + g2 * acc_ref[...].T


def _peer(h2, e1, nb, e2, r2, x1, mod, lw):
    B, S, _ = h2.shape
    tm, te = TM_PEER, TE_PEER
    rows = te // N_KEYS
    tok = pl.BlockSpec((1, tm, D), lambda b, i, e: (b, i, 0))
    slab = pl.BlockSpec((1, P_HEADS, rows, tm), lambda b, i, e: (b, 0, e, i))
    dense = pl.BlockSpec((1, P_HEADS, N_KEYS, tm), lambda b, i, e: (b, 0, 0, i))
    return pl.pallas_call(
        _peer_kernel,
        grid=(B, S // tm, N_EXPERTS // te),
        in_specs=[tok, slab, slab, dense, dense,
                  pl.BlockSpec((te, D), lambda b, i, e: (e, 0)),
                  pl.BlockSpec((D, te), lambda b, i, e: (0, e)),
                  tok, pl.BlockSpec((1, 1, 6 * D), lambda b, i, e: (b, 0, 0))],
        out_specs=tok,
        out_shape=jax.ShapeDtypeStruct((B, S, D), F32),
        scratch_shapes=[pltpu.VMEM((D, tm), F32),
                        pltpu.VMEM((te // SUB_PEER, 2 * P_HEADS * (SUB_PEER // N_KEYS), BF16_ROWS, tm), BF16),
                        pltpu.VMEM((te // SUB_PEER, SUB_PEER, tm), BF16),
                        pltpu.VMEM((2, SUB_PEER, tm), F32)],
        compiler_params=_params(("parallel", "parallel", "arbitrary")),
        name="peer",
    )(h2, e1, nb, e2, r2, lw["peer_u"], lw["peer_vt"], x1, mod)


def _block_ones(n, blk):
    i = jnp.arange(n) // blk
    return (i[:, None] == i[None, :]).astype(BF16)


def _pad_cols(w, n):
    return jnp.concatenate([w, jnp.zeros((w.shape[0], n - w.shape[1]), w.dtype)], axis=1)


def _layer_weights(l, norm1_g, norm2_g, w_in, a_q_norm, a_k_norm, cq_norm, ckv_norm, w_uq, w_ukv,
                   b_q_norm, b_k_norm, w_pa, w_pb, w_out, peer_wq, peer_k1, peer_k2, peer_u, peer_v):
    w = w_in[l]
    k0, k1 = w[:, 512:576], w[:, 576:640]
    v0, v1 = w[:, 640:704], w[:, 704:768]
    z64 = jnp.zeros((D, B_NOPE), F32)
    z32 = jnp.zeros((D, LANES - B_QK), F32)
    w_in_r = jnp.concatenate([w[:, 0:512], k0, k0, k1, k1, v0, v0, v1, v1, w[:, 768:1024], w[:, 1024:1280],
                              w[:, 1312:2336], w[:, 2336:3360], z64, w[:, 1280:1312], z32], axis=1).astype(BF16)
    uq = w_uq[l].reshape(Q_LORA, B_HEADS, B_QK)
    w_uq_r = jnp.pad(uq, ((0, 0), (0, 0), (0, LANES - B_QK))).reshape(Q_LORA, B_HEADS * LANES).astype(BF16)
    ukv = w_ukv[l].reshape(KV_LORA, B_HEADS, B_NOPE + B_V)
    w_uk_r = jnp.pad(ukv[:, :, :B_NOPE], ((0, 0), (0, 0), (0, LANES - B_NOPE))).reshape(KV_LORA, B_HEADS * LANES).astype(BF16)
    w_uv_r = ukv[:, :, B_NOPE:].reshape(KV_LORA, B_HEADS * B_V).astype(BF16)
    row = lambda a: a.reshape(1, -1)
    return {
        "norm1_g": row(norm1_g[l]), "norm2_g": row(norm2_g[l]), "w_in": w_in_r,
        "a_qg": row(jnp.tile(a_q_norm[l], A_HEADS)), "a_kg": row(jnp.tile(a_k_norm[l], 2 * A_KV)),
        "ones_q": _block_ones(A_HEADS * A_DH, A_DH), "ones_k": _block_ones(2 * A_KV * A_DH, A_DH),
        "cq_g": row(cq_norm[l]), "ckv_g": row(ckv_norm[l]),
        "w_uq": w_uq_r, "w_uk": w_uk_r, "w_uv": w_uv_r,
        "b_qg": row(_pad_cols(b_q_norm[l][None], LANES)), "b_kg": row(_pad_cols(b_k_norm[l][None], LANES)),
        "w_pa": w_pa[l].astype(BF16), "w_pb": w_pb[l].astype(BF16), "w_out": w_out[l].astype(BF16),
        "peer_wq": peer_wq[l].astype(BF16), "peer_k1": peer_k1[l].astype(BF16), "peer_k2": peer_k2[l].astype(BF16),
        "peer_u": peer_u[l].astype(BF16), "peer_vt": peer_v[l].astype(BF16).T,
    }


def _rope_tables(S):
    half = B_ROPE // 2
    inv = jnp.power(jnp.float32(ROPE_BASE), -jnp.arange(half, dtype=F32) / half)
    ang = jnp.arange(S, dtype=F32)[:, None] * inv[None, :]
    cos, sin = jnp.cos(ang), jnp.sin(ang)
    zeros = jnp.zeros((S, half), F32)
    pad = jnp.zeros((S, LANES - B_QK), F32)
    cos_t = jnp.concatenate([jnp.ones((S, B_NOPE), F32), cos, cos, pad + 1.0], axis=1)
    sin_lo = jnp.concatenate([jnp.zeros((S, B_NOPE), F32), -sin, zeros, pad], axis=1)
    sin_hi = jnp.concatenate([jnp.zeros((S, B_NOPE), F32), zeros, sin, pad], axis=1)
    return cos_t, sin_lo, sin_hi


def _trunk(x, mods, layers, sinks):
    rope = _rope_tables(x.shape[1])
    for l in range(DEPTH):
        lw = layers[l]
        mod = mods[l]
        qa, ka, va, qb, kb, vb, ga, gb = _inproj(x, mod, lw, rope)
        ya = _attn_a(qa, ka, va, sinks[l])
        yb = _attn_b(qb, kb, vb)
        x1, h2, scores = _mixout(x, ya, yb, ga, gb, mod, lw)
        e1, nb, e2, r2 = _route(scores)
        x = _peer(h2, e1, nb, e2, r2, x1, mod, lw)
    return x


def kernel(x_prompt, x_sample, c_prompt, c_sample, ada_w, ada_b, norm1_g, norm2_g, w_in, a_q_norm, a_k_norm, a_sink, cq_norm, ckv_norm, w_uq, w_ukv, b_q_norm, b_k_norm, w_pa, w_pb, w_out, peer_wq, peer_k1, peer_k2, peer_u, peer_v):
    nb_prompt = c_prompt.shape[0]
    mod_all = _ada(jnp.concatenate([c_prompt, c_sample], axis=0), ada_w, ada_b)
    mods_p = [mod_all[l, :nb_prompt].reshape(nb_prompt, 1, 6 * D) for l in range(DEPTH)]
    mods_s = [mod_all[l, nb_prompt:].reshape(-1, 1, 6 * D) for l in range(DEPTH)]
    layers = [_layer_weights(l, norm1_g, norm2_g, w_in, a_q_norm, a_k_norm, cq_norm, ckv_norm, w_uq, w_ukv,
                             b_q_norm, b_k_norm, w_pa, w_pb, w_out, peer_wq, peer_k1, peer_k2, peer_u, peer_v)
              for l in range(DEPTH)]
    sinks = [a_sink[l] for l in range(DEPTH)]
    y_prompt = _trunk(x_prompt, mods_p, layers, sinks)
    y_sample = _trunk(x_sample, mods_s, layers, sinks)
    return (y_prompt, y_sample)
```

```python
import functools
import math

import jax
import jax.numpy as jnp
from jax import lax
from jax.experimental import pallas as pl
from jax.experimental.pallas import tpu as pltpu

F32 = jnp.float32
BF16 = jnp.bfloat16

D = 1024
DEPTH = 2
EPS = 1e-6
NEG_INF = -1e30
LOG2E = 1.4426950408889634
LANES = 128
BF16_ROWS = 16
BLOCK = 128
A_HEADS, A_KV, A_DH = 8, 2, 64
WINDOW = 128
B_HEADS, B_NOPE, B_ROPE, B_V = 8, 64, 32, 64
B_QK = B_NOPE + B_ROPE
Q_LORA = KV_LORA = 256
ROPE_BASE = 10000.0
P_HEADS, P_HALF, N_KEYS, P_TOPK = 8, 128, 128, 16
N_EXPERTS = N_KEYS * N_KEYS

C_QA, C_KA, C_VA, C_CQ, C_CKV, C_GA, C_GB, C_KR, N_IN = 0, 512, 768, 1024, 1280, 1536, 2560, 3584, 3712

VMEM_LIMIT = 56 * 1024 * 1024

TM_IN = 512
TM_MIX = 512
TQ_B = 512
TK_B = 512
TL_ROUTE = 128
TM_PEER = 512
TE_PEER = 2048
SUB_PEER = 512


def _dot(a, b):
    return jnp.dot(a, b, preferred_element_type=F32)


def _dot_nt(a, b):
    return lax.dot_general(a, b, (((1,), (1,)), ((), ())), preferred_element_type=F32)


def _params(sem):
    return pltpu.CompilerParams(dimension_semantics=sem, vmem_limit_bytes=VMEM_LIMIT)


def _ada_kernel(c_ref, w_ref, b_ref, o_ref):
    c = c_ref[...]
    s = c * jax.nn.sigmoid(c)
    w = w_ref[0]
    s_hi = s.astype(BF16)
    s_lo = (s - s_hi.astype(F32)).astype(BF16)
    w_hi = w.astype(BF16)
    w_lo = (w - w_hi.astype(F32)).astype(BF16)
    o_ref[0] = _dot(s_hi, w_hi) + _dot(s_lo, w_hi) + _dot(s_hi, w_lo) + b_ref[0]


def _ada(c_all, ada_w, ada_b):
    n = c_all.shape[0]
    tn = 1536
    return pl.pallas_call(
        _ada_kernel,
        grid=(DEPTH, 6 * D // tn),
        in_specs=[
            pl.BlockSpec((n, D), lambda l, j: (0, 0)),
            pl.BlockSpec((1, D, tn), lambda l, j: (l, 0, j)),
            pl.BlockSpec((1, 1, tn), lambda l, j: (l, 0, j)),
        ],
        out_specs=pl.BlockSpec((1, n, tn), lambda l, j: (l, 0, j)),
        out_shape=jax.ShapeDtypeStruct((DEPTH, n, 6 * D), F32),
        compiler_params=_params(("arbitrary", "arbitrary")),
        name="ada",
    )(c_all, ada_w, ada_b.reshape(DEPTH, 1, 6 * D))


def _rope_tile(x, cos, sin_lo, sin_hi):
    return x * cos + pltpu.roll(x, LANES - B_ROPE // 2, 1) * sin_lo + pltpu.roll(x, B_ROPE // 2, 1) * sin_hi


def _inproj_kernel(x_ref, mod_ref, g1_ref, win_ref, aqg_ref, akg_ref, onesq_ref, onesk_ref,
                   cqg_ref, ckvg_ref, wuq_ref, wuk_ref, wuv_ref, bqg_ref, bkg_ref,
                   cos_ref, sinlo_ref, sinhi_ref,
                   qa_ref, ka_ref, va_ref, qb_ref, kb_ref, vb_ref, ga_ref, gb_ref):
    x = x_ref[0]
    sh1 = mod_ref[0, :, 0:D]
    sc1 = mod_ref[0, :, D:2 * D]
    ms = jnp.mean(x * x, axis=-1, keepdims=True)
    h = (x * lax.rsqrt(ms + EPS)) * g1_ref[...]
    h = h * (1.0 + sc1) + sh1
    hb = h.astype(BF16)

    def proj(a, b):
        return _dot(hb, win_ref[:, a:b])

    z = proj(C_QA, C_KA)
    ss = _dot((z * z).astype(BF16), onesq_ref[...])
    qa_ref[0] = (z * lax.rsqrt(ss * (1.0 / A_DH) + EPS) * aqg_ref[...] * (A_DH ** -0.5)).astype(BF16)
    z = proj(C_KA, C_VA)
    ss = _dot((z * z).astype(BF16), onesk_ref[...])
    ka_ref[0] = (z * lax.rsqrt(ss * (1.0 / A_DH) + EPS) * akg_ref[...]).astype(BF16)
    va_ref[0] = proj(C_VA, C_CQ).astype(BF16)

    cos = cos_ref[...]
    sin_lo = sinlo_ref[...]
    sin_hi = sinhi_ref[...]

    z = proj(C_CQ, C_CKV)
    ms = jnp.mean(z * z, axis=-1, keepdims=True)
    cqn = (z * lax.rsqrt(ms + EPS) * cqg_ref[...]).astype(BF16)
    q = _dot(cqn, wuq_ref[...])
    heads = range(B_HEADS)
    qs = [q[:, hd * LANES:(hd + 1) * LANES] for hd in heads]
    inv = [lax.rsqrt(jnp.sum(x * x, axis=-1, keepdims=True) * (1.0 / B_QK) + EPS) for x in qs]
    bqg = bqg_ref[...] * (B_QK ** -0.5 * LOG2E)
    qs = [x * r * bqg for x, r in zip(qs, inv)]
    qs = [_rope_tile(x, cos, sin_lo, sin_hi) for x in qs]
    for hd in heads:
        qb_ref[0, :, hd * LANES:(hd + 1) * LANES] = qs[hd].astype(BF16)

    z = proj(C_CKV, C_GA)
    ms = jnp.mean(z * z, axis=-1, keepdims=True)
    ckvn = (z * lax.rsqrt(ms + EPS) * ckvg_ref[...]).astype(BF16)
    kn = _dot(ckvn, wuk_ref[...])
    vb_ref[0] = _dot(ckvn, wuv_ref[...]).astype(BF16)
    kr = proj(C_KR, N_IN)
    bkg = bkg_ref[...]
    ks = [kn[:, hd * LANES:(hd + 1) * LANES] + kr for hd in heads]
    inv = [lax.rsqrt(jnp.sum(x * x, axis=-1, keepdims=True) * (1.0 / B_QK) + EPS) for x in ks]
    ks = [x * r * bkg for x, r in zip(ks, inv)]
    ks = [_rope_tile(x, cos, sin_lo, sin_hi) for x in ks]
    for hd in heads:
        kb_ref[0, :, hd * LANES:(hd + 1) * LANES] = ks[hd].astype(BF16)

    ga_ref[0] = proj(C_GA, C_GB).astype(BF16)
    gb_ref[0] = proj(C_GB, C_KR).astype(BF16)


def _inproj(x, mod, lw, rope):
    B, S, _ = x.shape
    tm = TM_IN
    tok = lambda w: pl.BlockSpec((1, tm, w), lambda b, i: (b, i, 0))
    full = lambda a: pl.BlockSpec(a.shape, lambda b, i: (0,) * a.ndim)
    rope_spec = pl.BlockSpec((tm, LANES), lambda b, i: (i, 0))
    consts = [lw["norm1_g"], lw["w_in"], lw["a_qg"], lw["a_kg"], lw["ones_q"], lw["ones_k"],
              lw["cq_g"], lw["ckv_g"], lw["w_uq"], lw["w_uk"], lw["w_uv"], lw["b_qg"], lw["b_kg"]]
    widths = [512, 256, 256, 1024, 1024, 512, 1024, 1024]
    return pl.pallas_call(
        _inproj_kernel,
        grid=(B, S // tm),
        in_specs=[tok(D), pl.BlockSpec((1, 1, 6 * D), lambda b, i: (b, 0, 0))]
                 + [full(a) for a in consts] + [rope_spec] * 3,
        out_specs=[tok(w) for w in widths],
        out_shape=[jax.ShapeDtypeStruct((B, S, w), BF16) for w in widths],
        compiler_params=_params(("parallel", "parallel")),
        name="inproj",
    )(x, mod, *consts, *rope)


def _attn_a_kernel(sink_ref, q_ref, k_ref, v_ref, o_ref, bias_ref):
    n = pl.program_id(1)
    S = k_ref.shape[1]
    span = 3 * BLOCK
    group = A_HEADS // A_KV
    start = pl.multiple_of(jnp.clip((n - 1) * BLOCK, 0, S - span), BLOCK)
    kw = k_ref[0, pl.ds(start, span), :]
    vw = v_ref[0, pl.ds(start, span), :]
    rows = group * BLOCK
    lane = lax.broadcasted_iota(jnp.int32, (BLOCK, LANES), 1)
    rowhead = lax.broadcasted_iota(jnp.int32, (rows, 1), 0) // BLOCK

    @pl.when((n <= 1) | (n == pl.num_programs(1) - 1))
    def _():
        qpos = n * BLOCK + lax.broadcasted_iota(jnp.int32, (rows, span), 0) % BLOCK
        kpos = start + lax.broadcasted_iota(jnp.int32, (rows, span), 1)
        dist = jnp.abs(qpos - kpos)
        distf = dist.astype(F32)
        for j in range(A_KV):
            slope = jnp.exp2(-8.0 * (group * j + rowhead + 1).astype(F32) / A_HEADS)
            bias_ref[j] = jnp.where(dist <= WINDOW, -slope * distf, NEG_INF)

    scores = []
    for j in range(A_KV):
        kd = kw[:, j * LANES:(j + 1) * LANES]
        parts = []
        for pp in range(group // 2):
            p = (group // 2) * j + pp
            qp = q_ref[0, :, p * LANES:(p + 1) * LANES]
            zero = jnp.zeros_like(qp)
            parts.append(jnp.where(lane < A_DH, qp, zero))
            parts.append(jnp.where(lane >= A_DH, qp, zero))
        scores.append(_dot_nt(jnp.concatenate(parts, axis=0), kd))

    for j in range(A_KV):
        vd = vw[:, j * LANES:(j + 1) * LANES]
        sink = jnp.zeros((rows, 1), F32)
        for g in range(group):
            sink = jnp.where(rowhead == g, sink_ref[group * j + g], sink)
        bias = bias_ref[j]
        logits = jnp.where(bias > 0.5 * NEG_INF, scores[j] + bias, NEG_INF)
        m = jnp.maximum(jnp.max(logits, axis=-1, keepdims=True), sink)
        p = jnp.exp(logits - m)
        denom = jnp.sum(p, axis=-1, keepdims=True) + jnp.exp(sink - m)
        o = _dot(p.astype(BF16), vd) / denom
        for pp in range(group // 2):
            oe = o[(2 * pp) * BLOCK:(2 * pp + 1) * BLOCK]
            oo = o[(2 * pp + 1) * BLOCK:(2 * pp + 2) * BLOCK]
            p_idx = (group // 2) * j + pp
            o_ref[0, :, p_idx * LANES:(p_idx + 1) * LANES] = jnp.where(lane < A_DH, oe, oo).astype(BF16)


def _attn_a(qa, ka, va, sink):
    B, S, _ = qa.shape
    return pl.pallas_call(
        _attn_a_kernel,
        grid=(B, S // BLOCK),
        in_specs=[
            pl.BlockSpec(memory_space=pltpu.SMEM),
            pl.BlockSpec((1, BLOCK, 512), lambda b, n: (b, n, 0)),
            pl.BlockSpec((1, S, 256), lambda b, n: (b, 0, 0)),
            pl.BlockSpec((1, S, 256), lambda b, n: (b, 0, 0)),
        ],
        out_specs=pl.BlockSpec((1, BLOCK, 512), lambda b, n: (b, n, 0)),
        out_shape=jax.ShapeDtypeStruct((B, S, 512), BF16),
        scratch_shapes=[pltpu.VMEM((A_KV, (A_HEADS // A_KV) * BLOCK, 3 * BLOCK), F32)],
        compiler_params=_params(("arbitrary", "arbitrary")),
        name="attn_a",
    )(sink, qa, ka, va)


def _attn_b_kernel(q_ref, k_ref, v_ref, o_ref, s_ref, p_ref):
    S = k_ref.shape[1]
    tq = q_ref.shape[1]
    nk = S // TK_B
    lane = lax.broadcasted_iota(jnp.int32, (tq, LANES), 1)
    outs = []
    for hh in range(2):
        q = q_ref[0, :, hh * LANES:(hh + 1) * LANES]

        def kchunk(j, hh=hh):
            return k_ref[0, j * TK_B:(j + 1) * TK_B, hh * LANES:(hh + 1) * LANES]

        def vchunk(j):
            return v_ref[0, j * TK_B:(j + 1) * TK_B, :]

        s_ref[0] = _dot_nt(q, kchunk(0))
        m = l = alpha = acc = None
        for j in range(nk):
            slot = j % 2
            if j + 1 < nk:
                s_ref[1 - slot] = _dot_nt(q, kchunk(j + 1))
            if j >= 1:
                pv = _dot(p_ref[1 - slot], vchunk(j - 1))
                acc = pv if acc is None else acc * alpha + pv
            row_max = jnp.max(s_ref[slot], axis=-1, keepdims=True)
            m_new = row_max if m is None else jnp.maximum(m, row_max)
            p = jnp.exp2(s_ref[slot] - m_new)
            p_ref[slot] = p.astype(BF16)
            psum = p[:, 0:LANES]
            for c in range(1, TK_B // LANES):
                psum = psum + p[:, c * LANES:(c + 1) * LANES]
            alpha = None if m is None else jnp.exp2(m - m_new)
            l = psum if l is None else alpha * l + psum
            m = m_new
        pv = _dot(p_ref[(nk - 1) % 2], vchunk(nk - 1))
        acc = pv if acc is None else acc * alpha + pv
        outs.append(acc / jnp.sum(l, axis=-1, keepdims=True))
    o_ref[0] = jnp.where(lane < B_V, outs[0], outs[1]).astype(BF16)


def _attn_b(qb, kb, vb):
    B, S, _ = qb.shape
    tq = TQ_B
    return pl.pallas_call(
        _attn_b_kernel,
        grid=(B, B_HEADS // 2, S // tq),
        in_specs=[
            pl.BlockSpec((1, tq, 2 * LANES), lambda b, p, i: (b, i, p)),
            pl.BlockSpec((1, S, 2 * LANES), lambda b, p, i: (b, 0, p)),
            pl.BlockSpec((1, S, LANES), lambda b, p, i: (b, 0, p)),
        ],
        out_specs=pl.BlockSpec((1, tq, LANES), lambda b, p, i: (b, i, p)),
        out_shape=jax.ShapeDtypeStruct((B, S, B_HEADS * B_V), BF16),
        scratch_shapes=[pltpu.VMEM((2, tq, TK_B), F32), pltpu.VMEM((2, tq, TK_B), BF16)],
        compiler_params=_params(("parallel", "parallel", "arbitrary")),
        name="attn_b",
    )(qb, kb, vb)


def _mixout_kernel(x_ref, ya_ref, yb_ref, ga_ref, gb_ref, mod_ref, wpa_ref, wpb_ref, wout_ref,
                   g2_ref, wq_ref, k1_ref, k2_ref, x1_ref, h2_ref, sc_ref):
    ma = _dot(ya_ref[0], wpa_ref[...])
    mb = _dot(yb_ref[0], wpb_ref[...])
    m = jax.nn.sigmoid(ga_ref[0].astype(F32)) * ma + jax.nn.sigmoid(gb_ref[0].astype(F32)) * mb
    mo = _dot(m.astype(BF16), wout_ref[...])
    g1 = mod_ref[0, :, 2 * D:3 * D]
    sh2 = mod_ref[0, :, 3 * D:4 * D]
    sc2 = mod_ref[0, :, 4 * D:5 * D]
    x1 = x_ref[0] + g1 * mo
    x1_ref[0] = x1
    ms = jnp.mean(x1 * x1, axis=-1, keepdims=True)
    h2 = (x1 * lax.rsqrt(ms + EPS)) * g2_ref[...]
    h2 = (h2 * (1.0 + sc2) + sh2).astype(BF16)
    h2_ref[0] = h2
    q = _dot(h2, wq_ref[...]).astype(BF16)
    for hd in range(P_HEADS):
        for c, kref in enumerate((k1_ref, k2_ref)):
            col = (2 * hd + c) * P_HALF
            sc_ref[0, 2 * hd + c] = _dot_nt(kref[...], q[:, col:col + P_HALF])


def _mixout(x, ya, yb, ga, gb, mod, lw):
    B, S, _ = x.shape
    tm = TM_MIX
    tok = lambda w: pl.BlockSpec((1, tm, w), lambda b, i: (b, i, 0))
    full = lambda a: pl.BlockSpec(a.shape, lambda b, i: (0,) * a.ndim)
    consts = [lw["w_pa"], lw["w_pb"], lw["w_out"], lw["norm2_g"], lw["peer_wq"], lw["peer_k1"], lw["peer_k2"]]
    return pl.pallas_call(
        _mixout_kernel,
        grid=(B, S // tm),
        in_specs=[tok(D), tok(512), tok(512), tok(D), tok(D),
                  pl.BlockSpec((1, 1, 6 * D), lambda b, i: (b, 0, 0))] + [full(a) for a in consts],
        out_specs=[tok(D), tok(D), pl.BlockSpec((1, 2 * P_HEADS, N_KEYS, tm), lambda b, i: (b, 0, 0, i))],
        out_shape=[jax.ShapeDtypeStruct((B, S, D), F32), jax.ShapeDtypeStruct((B, S, D), BF16),
                   jax.ShapeDtypeStruct((B, 2 * P_HEADS, N_KEYS, S), F32)],
        compiler_params=_params(("parallel", "parallel")),
        name="mixout",
    )(x, ya, yb, ga, gb, mod, *consts)


def _extract_topk(s, key_iota):
    rank = jnp.full(s.shape, float(P_TOPK), F32)
    vals = []
    for a in range(P_TOPK):
        m = jnp.max(s, axis=0, keepdims=True)
        idx = jnp.min(jnp.where(s == m, key_iota, float(N_KEYS)), axis=0, keepdims=True)
        hit = key_iota == idx
        rank = jnp.where(hit, float(a), rank)
        s = jnp.where(hit, -jnp.inf, s)
        vals.append(m)
    return vals, rank


def _count(mask):
    return jnp.sum(jnp.where(mask, 1.0, 0.0), axis=0, keepdims=True)


def _pair_tiles(v1, v2, b8):
    v2lo = jnp.concatenate(v2[:8], axis=0)
    v2hi = jnp.concatenate(v2[8:], axis=0)
    v1hi = jnp.concatenate(v1[8:], axis=0)
    tiles = [v1[0] + v2lo, v1[0] + v2hi, v1[1] + v2lo]
    flats = [b8, b8 + 8.0, b8 + float(P_TOPK)]
    valid = [None, None, None]
    for a in range(2, 8):
        ok = b8 < float(P_TOPK // (a + 1))
        tiles.append(jnp.where(ok, v1[a] + v2lo, -jnp.inf))
        flats.append(b8 + float(a * P_TOPK))
        valid.append(ok)
    tiles.append(v1hi + v2[0])
    flats.append((b8 + 8.0) * float(P_TOPK))
    valid.append(None)
    return tiles, flats, valid


def _pair_step(tiles, flats, break_ties):
    m = tiles[0]
    for t in tiles[1:]:
        m = jnp.maximum(m, t)
    m = jnp.max(m, axis=0, keepdims=True)
    if break_ties:
        big = float(P_TOPK * P_TOPK)
        idx = None
        for t, f in zip(tiles, flats):
            cur = jnp.where(t == m, f, big)
            idx = cur if idx is None else jnp.minimum(idx, cur)
        idx = jnp.min(idx, axis=0, keepdims=True)
        tiles = [jnp.where(f == idx, -jnp.inf, t) for t, f in zip(tiles, flats)]
    else:
        tiles = [jnp.where(t == m, -jnp.inf, t) for t in tiles]
    return tiles, m


def _pair_counts(tiles, valid):
    taken = [t == -jnp.inf if ok is None else (t == -jnp.inf) & ok for t, ok in zip(tiles, valid)]
    nb = [_count(taken[0]) + _count(taken[1]), _count(taken[2])] + [_count(taken[a + 1]) for a in range(2, 8)]
    last = jnp.where(taken[9], 1.0, 0.0)
    nb += [last[a:a + 1] for a in range(8)]
    total = nb[0]
    for x in nb[1:]:
        total = total + x
    return nb, total


def _key_next(s, m):
    return jnp.max(s if m is None else jnp.where(s < m, s, -jnp.inf), axis=0, keepdims=True)


def _route_fast(sc_ref, e1_ref, nb_ref, e2_ref, r2_ref, b8):
    tl = sc_ref.shape[3]
    want = float(P_TOPK)
    bad = jnp.zeros((1, tl), jnp.bool_)
    v1, v2, m1, m2 = [], [], None, None
    for a in range(P_TOPK):
        m1 = _key_next(sc_ref[0, 0], m1)
        m2 = _key_next(sc_ref[0, 1], m2)
        v1.append(m1)
        v2.append(m2)
    for hd in range(P_HEADS):
        tiles, flats, valid = _pair_tiles(v1, v2, b8)
        more = hd + 1 < P_HEADS
        w1, w2, m1, m2 = [], [], None, None
        m0 = None
        z = jnp.zeros((1, tl), F32)
        for k in range(P_TOPK):
            tiles, m = _pair_step(tiles, flats, False)
            if m0 is None:
                m0 = m
            z = z + jnp.exp(m - m0)
            if more:
                m1 = _key_next(sc_ref[0, 2 * hd + 2], m1)
                m2 = _key_next(sc_ref[0, 2 * hd + 3], m2)
                w1.append(m1)
                w2.append(m2)
        nb, total = _pair_counts(tiles, valid)
        o1 = sc_ref[0, 2 * hd]
        o2 = sc_ref[0, 2 * hd + 1]
        bad = bad | (_count(o1 >= v1[-1]) != want) | (_count(o2 >= v2[-1]) != want) | (total != want)
        nb1 = jnp.zeros((N_KEYS, tl), F32)
        rank2 = jnp.zeros((N_KEYS, tl), F32)
        for a in range(P_TOPK):
            nb1 = jnp.where(o1 == v1[a], nb[a], nb1)
            rank2 = jnp.where(v2[a] > o2, float(a + 1), rank2)
        e1_ref[0, hd] = jnp.exp(o1 - v1[0]) / (z + z)
        nb_ref[0, hd] = nb1
        e2_ref[0, hd] = jnp.exp(o2 - v2[0]).astype(BF16)
        r2_ref[0, hd] = rank2.astype(BF16)
        if more:
            v1, v2 = w1, w2
    return bad


def _route_head(s1, s2, key_iota, b8):
    tl = s1.shape[1]
    v1, rank1 = _extract_topk(s1, key_iota)
    v2, rank2 = _extract_topk(s2, key_iota)
    tiles, flats, valid = _pair_tiles(v1, v2, b8)
    m0 = None
    z = jnp.zeros((1, tl), F32)
    for k in range(P_TOPK):
        tiles, m = _pair_step(tiles, flats, True)
        if m0 is None:
            m0 = m
        z = z + jnp.exp(m - m0)
    nb, _ = _pair_counts(tiles, valid)
    nb1 = jnp.zeros((N_KEYS, tl), F32)
    for a in range(P_TOPK):
        nb1 = jnp.where(rank1 == float(a), nb[a], nb1)
    e1 = jnp.exp(s1 - v1[0]) / (z + z)
    e2 = jnp.exp(s2 - v2[0])
    return e1, nb1, e2, rank2


def _route_kernel(sc_ref, e1_ref, nb_ref, e2_ref, r2_ref):
    tl = sc_ref.shape[3]
    key_iota = lax.broadcasted_iota(jnp.int32, (N_KEYS, tl), 0).astype(F32)
    b8 = lax.broadcasted_iota(jnp.int32, (8, tl), 0).astype(F32)

    bad = _route_fast(sc_ref, e1_ref, nb_ref, e2_ref, r2_ref, b8)

    @pl.when(jnp.max(jnp.where(bad, 1.0, 0.0)) > 0.0)
    def _():
        def head(hd, carry):
            e1, nb1, e2, rank2 = _route_head(sc_ref[0, 2 * hd], sc_ref[0, 2 * hd + 1], key_iota, b8)
            e1_ref[0, hd] = e1
            nb_ref[0, hd] = nb1
            e2_ref[0, hd] = e2.astype(BF16)
            r2_ref[0, hd] = rank2.astype(BF16)
            return carry

        lax.fori_loop(0, P_HEADS, head, 0)


def _route(scores):
    B, _, _, S = scores.shape
    tl = TL_ROUTE
    spec = pl.BlockSpec((1, P_HEADS, N_KEYS, tl), lambda b, i: (b, 0, 0, i))
    shp = lambda dt: jax.ShapeDtypeStruct((B, P_HEADS, N_KEYS, S), dt)
    return pl.pallas_call(
        _route_kernel,
        grid=(B, S // tl),
        in_specs=[pl.BlockSpec((1, 2 * P_HEADS, N_KEYS, tl), lambda b, i: (b, 0, 0, i))],
        out_specs=[spec] * 4,
        out_shape=[shp(F32), shp(F32), shp(BF16), shp(BF16)],
        compiler_params=_params(("parallel", "parallel")),
        name="route",
    )(scores)


def _peer_kernel(h_ref, e1_ref, nb_ref, e2_ref, r2_ref, u_ref, vt_ref, x1_ref, mod_ref, y_ref,
                 acc_ref, bc_ref, mt_ref, a_ref):
    e = pl.program_id(2)
    rows_sub = SUB_PEER // N_KEYS
    tm = h_ref.shape[1]

    @pl.when(e == 0)
    def _():
        acc_ref[...] = jnp.zeros_like(acc_ref)

    h = h_ref[0]
    zero = jnp.zeros((BF16_ROWS, tm), BF16)
    nsub = TE_PEER // SUB_PEER

    def scores(k):
        return _dot_nt(u_ref[k * SUB_PEER:(k + 1) * SUB_PEER, :], h)

    a_ref[0] = scores(0)
    for k in range(nsub):
        if k >= 1:
            acc_ref[...] += _dot(vt_ref[:, (k - 1) * SUB_PEER:k * SUB_PEER], mt_ref[k - 1])
        if k + 1 < nsub:
            a_ref[(k + 1) % 2] = scores(k + 1)

        for hd in range(P_HEADS):
            for r in range(rows_sub):
                idx = 2 * (hd * rows_sub + r)
                row = k * rows_sub + r
                bc_ref[k, idx] = jnp.broadcast_to(nb_ref[0, hd, row:row + 1, :], (BF16_ROWS, tm)).astype(BF16)
                bc_ref[k, idx + 1] = jnp.broadcast_to(e1_ref[0, hd, row:row + 1, :], (BF16_ROWS, tm)).astype(BF16)

        for ib in range(N_KEYS // BF16_ROWS):
            lo = ib * BF16_ROWS
            w = [None] * rows_sub
            for hd in range(P_HEADS):
                r2 = r2_ref[0, hd, lo:lo + BF16_ROWS, :]
                e2 = e2_ref[0, hd, lo:lo + BF16_ROWS, :]
                for r in range(rows_sub):
                    idx = 2 * (hd * rows_sub + r)
                    t = jnp.where(r2 < bc_ref[k, idx], e2, zero) * bc_ref[k, idx + 1]
                    w[r] = t if w[r] is None else w[r] + t
            for r in range(rows_sub):
                ar = a_ref[k % 2, r * N_KEYS + lo:r * N_KEYS + lo + BF16_ROWS, :]
                g = ar * (1.0 + lax.erf(ar * (1.0 / math.sqrt(2.0))))
                mt_ref[k, r * N_KEYS + lo:r * N_KEYS + lo + BF16_ROWS, :] = w[r] * g.astype(BF16)
    acc_ref[...] += _dot(vt_ref[:, (nsub - 1) * SUB_PEER:nsub * SUB_PEER], mt_ref[nsub - 1])

    @pl.when(e == pl.num_programs(2) - 1)
    def _():
        g2 = mod_ref[0, :, 5 * D:6 * D]
        y_ref[0] = x1_ref[0] + g2 * acc_ref[...].T


def _peer(h2, e1, nb, e2, r2, x1, mod, lw):
    B, S, _ = h2.shape
    tm, te = TM_PEER, TE_PEER
    rows = te // N_KEYS
    tok = pl.BlockSpec((1, tm, D), lambda b, i, e: (b, i, 0))
    slab = pl.BlockSpec((1, P_HEADS, rows, tm), lambda b, i, e: (b, 0, e, i))
    dense = pl.BlockSpec((1, P_HEADS, N_KEYS, tm), lambda b, i, e: (b, 0, 0, i))
    return pl.pallas_call(
        _peer_kernel,
        grid=(B, S // tm, N_EXPERTS // te),
        in_specs=[tok, slab, slab, dense, dense,
                  pl.BlockSpec((te, D), lambda b, i, e: (e, 0)),
                  pl.BlockSpec((D, te), lambda b, i, e: (0, e)),
                  tok, pl.BlockSpec((1, 1, 6 * D), lambda b, i, e: (b, 0, 0))],
        out_specs=tok,
        out_shape=jax.ShapeDtypeStruct((B, S, D), F32),
        scratch_shapes=[pltpu.VMEM((D, tm), F32),
                        pltpu.VMEM((te // SUB_PEER, 2 * P_HEADS * (SUB_PEER // N_KEYS), BF16_ROWS, tm), BF16),
                        pltpu.VMEM((te // SUB_PEER, SUB_PEER, tm), BF16),
                        pltpu.VMEM((2, SUB_PEER, tm), F32)],
        compiler_params=_params(("parallel", "parallel", "arbitrary")),
        name="peer",
    )(h2, e1, nb, e2, r2, lw["peer_u"], lw["peer_vt"], x1, mod)


def _block_ones(n, blk):
    i = jnp.arange(n) // blk
    return (i[:, None] == i[None, :]).astype(BF16)


def _pad_cols(w, n):
    return jnp.concatenate([w, jnp.zeros((w.shape[0], n - w.shape[1]), w.dtype)], axis=1)


def _layer_weights(l, norm1_g, norm2_g, w_in, a_q_norm, a_k_norm, cq_norm, ckv_norm, w_uq, w_ukv,
                   b_q_norm, b_k_norm, w_pa, w_pb, w_out, peer_wq, peer_k1, peer_k2, peer_u, peer_v):
    w = w_in[l]
    k0, k1 = w[:, 512:576], w[:, 576:640]
    v0, v1 = w[:, 640:704], w[:, 704:768]
    z64 = jnp.zeros((D, B_NOPE), F32)
    z32 = jnp.zeros((D, LANES - B_QK), F32)
    w_in_r = jnp.concatenate([w[:, 0:512], k0, k0, k1, k1, v0, v0, v1, v1, w[:, 768:1024], w[:, 1024:1280],
                              w[:, 1312:2336], w[:, 2336:3360], z64, w[:, 1280:1312], z32], axis=1).astype(BF16)
    uq = w_uq[l].reshape(Q_LORA, B_HEADS, B_QK)
    w_uq_r = jnp.pad(uq, ((0, 0), (0, 0), (0, LANES - B_QK))).reshape(Q_LORA, B_HEADS * LANES).astype(BF16)
    ukv = w_ukv[l].reshape(KV_LORA, B_HEADS, B_NOPE + B_V)
    w_uk_r = jnp.pad(ukv[:, :, :B_NOPE], ((0, 0), (0, 0), (0, LANES - B_NOPE))).reshape(KV_LORA, B_HEADS * LANES).astype(BF16)
    w_uv_r = ukv[:, :, B_NOPE:].reshape(KV_LORA, B_HEADS * B_V).astype(BF16)
    row = lambda a: a.reshape(1, -1)
    return {
        "norm1_g": row(norm1_g[l]), "norm2_g": row(norm2_g[l]), "w_in": w_in_r,
        "a_qg": row(jnp.tile(a_q_norm[l], A_HEADS)), "a_kg": row(jnp.tile(a_k_norm[l], 2 * A_KV)),
        "ones_q": _block_ones(A_HEADS * A_DH, A_DH), "ones_k": _block_ones(2 * A_KV * A_DH, A_DH),
        "cq_g": row(cq_norm[l]), "ckv_g": row(ckv_norm[l]),
        "w_uq": w_uq_r, "w_uk": w_uk_r, "w_uv": w_uv_r,
        "b_qg": row(_pad_cols(b_q_norm[l][None], LANES)), "b_kg": row(_pad_cols(b_k_norm[l][None], LANES)),
        "w_pa": w_pa[l].astype(BF16), "w_pb": w_pb[l].astype(BF16), "w_out": w_out[l].astype(BF16),
        "peer_wq": peer_wq[l].astype(BF16), "peer_k1": peer_k1[l].astype(BF16), "peer_k2": peer_k2[l].astype(BF16),
        "peer_u": peer_u[l].astype(BF16), "peer_vt": peer_v[l].astype(BF16).T,
    }


def _rope_tables(S):
    half = B_ROPE // 2
    inv = jnp.power(jnp.float32(ROPE_BASE), -jnp.arange(half, dtype=F32) / half)
    ang = jnp.arange(S, dtype=F32)[:, None] * inv[None, :]
    cos, sin = jnp.cos(ang), jnp.sin(ang)
    zeros = jnp.zeros((S, half), F32)
    pad = jnp.zeros((S, LANES - B_QK), F32)
    cos_t = jnp.concatenate([jnp.ones((S, B_NOPE), F32), cos, cos, pad + 1.0], axis=1)
    sin_lo = jnp.concatenate([jnp.zeros((S, B_NOPE), F32), -sin, zeros, pad], axis=1)
    sin_hi = jnp.concatenate([jnp.zeros((S, B_NOPE), F32), zeros, sin, pad], axis=1)
    return cos_t, sin_lo, sin_hi


def _trunk(x, mods, layers, sinks):
    rope = _rope_tables(x.shape[1])
    for l in range(DEPTH):
        lw = layers[l]
        mod = mods[l]
        qa, ka, va, qb, kb, vb, ga, gb = _inproj(x, mod, lw, rope)
        ya = _attn_a(qa, ka, va, sinks[l])
        yb = _attn_b(qb, kb, vb)
        x1, h2, scores = _mixout(x, ya, yb, ga, gb, mod, lw)
        e1, nb, e2, r2 = _route(scores)
        x = _peer(h2, e1, nb, e2, r2, x1, mod, lw)
    return x


def kernel(x_prompt, x_sample, c_prompt, c_sample, ada_w, ada_b, norm1_g, norm2_g, w_in, a_q_norm, a_k_norm, a_sink, cq_norm, ckv_norm, w_uq, w_ukv, b_q_norm, b_k_norm, w_pa, w_pb, w_out, peer_wq, peer_k1, peer_k2, peer_u, peer_v):
    nb_prompt = c_prompt.shape[0]
    mod_all = _ada(jnp.concatenate([c_prompt, c_sample], axis=0), ada_w, ada_b)
    mods_p = [mod_all[l, :nb_prompt].reshape(nb_prompt, 1, 6 * D) for l in range(DEPTH)]
    mods_s = [mod_all[l, nb_prompt:].reshape(-1, 1, 6 * D) for l in range(DEPTH)]
    layers = [_layer_weights(l, norm1_g, norm2_g, w_in, a_q_norm, a_k_norm, cq_norm, ckv_norm, w_uq, w_ukv,
                             b_q_norm, b_k_norm, w_pa, w_pb, w_out, peer_wq, peer_k1, peer_k2, peer_u, peer_v)
              for l in range(DEPTH)]
    sinks = [a_sink[l] for l in range(DEPTH)]
    y_prompt = _trunk(x_prompt, mods_p, layers, sinks)
    y_sample = _trunk(x_sample, mods_s, layers, sinks)
    return (y_prompt, y_sample)
```
